```python
import math
import jax, jax.numpy as jnp
from jax import lax
import numpy as np

D_MODEL = 1024
BATCH = 32
SEQ = 2048
DEPTH = 1
DEC_BATCH = 8
DEC_SEQ = 64
PAST_LEN = 4096

CHUNK = 64
N_MOD = 9
D_FF = 2816
GMLP_CHUNK = 128
D_A = D_MODEL
GMLP_GROUPS = 8
GMLP_GROUP_DIM = D_A // GMLP_GROUPS
MLA_HEADS = 8
QK_NOPE = 64
QK_ROPE = 32
QK_HEAD = QK_NOPE + QK_ROPE
V_HEAD = 64
D_B = MLA_HEADS * V_HEAD
Q_RANK = 384
KV_RANK = 256
ROPE_THETA = 10000.0
Q_BLOCK = 128
EPS = 1e-6
NEG = -1e30
IN_SPLITS = (D_A, D_A, Q_RANK, KV_RANK, QK_ROPE, 2 * D_MODEL)
N_IN = 2 * D_A + Q_RANK + KV_RANK + QK_ROPE + 2 * D_MODEL

kernel_name = 'hybrid_gmlp_mla_macaron_streaming_step'


def rms_norm(x, g):
    x32 = x.astype(jnp.float32)
    y = x32 * lax.rsqrt(jnp.mean(x32 * x32, axis=-1, keepdims=True) + EPS)
    return (y * g.astype(jnp.float32)).astype(x.dtype)


def modulate(x, g, shift, scale):
    return rms_norm(x, g) * (1 + scale) + shift


def rope(x, pos):
    half = x.shape[-1] // 2
    freqs = ROPE_THETA ** (-jnp.arange(half, dtype=jnp.float32) / half)
    ang = pos[:, None] * freqs[None, :]
    cos = jnp.cos(ang)[None, :, None, :]
    sin = jnp.sin(ang)[None, :, None, :]
    x32 = x.astype(jnp.float32)
    x1, x2 = x32[..., :half], x32[..., half:]
    return jnp.concatenate([x1 * cos - x2 * sin, x1 * sin + x2 * cos], axis=-1).astype(x.dtype)


def split_cols(z, sizes):
    parts = []
    o = 0
    for s in sizes:
        parts.append(z[..., o:o + s])
        o += s
    return parts


def swiglu(h, w_up, w_down):
    gu = h @ w_up
    return (jax.nn.silu(gu[..., :D_FF]) * gu[..., D_FF:]) @ w_down


def gmlp_branch(z_u, z_v, g_v, ws, b):
    u = jax.nn.gelu(z_u)
    v = rms_norm(jax.nn.gelu(z_v), g_v)
    B, T, _ = v.shape
    L = min(T, GMLP_CHUNK)
    n = T // L
    mask = jnp.tril(jnp.ones((L, L), dtype=bool))
    w = jnp.where(mask[None], ws[:, :L, :L], 0).astype(v.dtype)
    vg = v.reshape(B, n, L, GMLP_GROUPS, GMLP_GROUP_DIM)
    mixed = jnp.einsum('gts,bnsgc->bntgc', w, vg) + b[:, :L].T[None, None, :, :, None]
    return u * mixed.reshape(B, T, D_A), v


def mla_queries(z_q, pos, p):
    B, T, _ = z_q.shape
    q = (rms_norm(z_q, p['g_q_lat']) @ p['w_uq']).reshape(B, T, MLA_HEADS, QK_HEAD)
    q = jnp.concatenate([q[..., :QK_NOPE], rope(q[..., QK_NOPE:], pos)], axis=-1)
    return rms_norm(q, p['g_qnorm'])


def mla_keys_values(ckv, krope, p):
    B, T, _ = ckv.shape
    k_nope = jnp.einsum('btr,rhd->bthd', ckv, p['w_uk'].reshape(KV_RANK, MLA_HEADS, QK_NOPE))
    k_rope = jnp.broadcast_to(krope[:, :, None, :], (B, T, MLA_HEADS, QK_ROPE))
    k = rms_norm(jnp.concatenate([k_nope, k_rope], axis=-1), p['g_knorm'])
    v = jnp.einsum('btr,rhd->bthd', ckv, p['w_uv'].reshape(KV_RANK, MLA_HEADS, V_HEAD))
    return k, v


def attend_block_causal(q, k, v):
    B, T, H, _ = q.shape
    nq = T // Q_BLOCK
    scale = QK_HEAD ** -0.5
    qb = q.reshape(B, nq, Q_BLOCK, H, QK_HEAD).transpose(1, 0, 2, 3, 4)
    kchunk = jnp.arange(T) // CHUNK
    qchunk = kchunk.reshape(nq, Q_BLOCK)

    def block(args):
        qi, qc = args
        s = jnp.einsum('bqhd,bkhd->bhqk', qi, k, preferred_element_type=jnp.float32) * scale
        s = jnp.where(kchunk[None, None, None, :] <= qc[None, None, :, None], s, NEG)
        pr = jax.nn.softmax(s, axis=-1).astype(v.dtype)
        return jnp.einsum('bhqk,bkhd->bqhd', pr, v)

    o = lax.map(block, (qb, qchunk))
    return o.transpose(1, 0, 2, 3, 4).reshape(B, T, H * V_HEAD)


def attend_all(q, k, v):
    B, T, H, _ = q.shape
    s = jnp.einsum('bqhd,bkhd->bhqk', q, k, preferred_element_type=jnp.float32) * (QK_HEAD ** -0.5)
    pr = jax.nn.softmax(s, axis=-1).astype(v.dtype)
    return jnp.einsum('bhqk,bkhd->bqhd', pr, v).reshape(B, T, H * V_HEAD)


def layer(x, c, pos, past_ckv, past_krope, p):
    B, T, _ = x.shape
    m = (jax.nn.silu(c) @ p['w_mod'] + p['b_mod']).reshape(B, N_MOD, D_MODEL)[:, :, None, :]
    sh1, sc1, gt1, sh2, sc2, gt2, sh3, sc3, gt3 = [m[:, i] for i in range(N_MOD)]
    x = x + 0.5 * gt1 * swiglu(modulate(x, p['g_ffn1'], sh1, sc1), p['w_ffn1_up'], p['w_ffn1_down'])
    h = modulate(x, p['g_mix'], sh2, sc2)
    z = h @ p['w_in']
    z_u, z_v, z_q, z_kv, z_kr, z_g = split_cols(z, IN_SPLITS)
    o_a, v_a = gmlp_branch(z_u, z_v, p['g_gmlp_v'], p['gmlp_ws'], p['gmlp_b'])
    q = mla_queries(z_q, pos, p)
    ckv = rms_norm(z_kv, p['g_kv_lat'])
    krope = rope(z_kr[:, :, None, :], pos)[:, :, 0, :]
    if past_ckv is None:
        k, v = mla_keys_values(ckv, krope, p)
        o_b = attend_block_causal(q, k, v)
    else:
        k, v = mla_keys_values(jnp.concatenate([past_ckv, ckv], axis=1),
                               jnp.concatenate([past_krope, krope], axis=1), p)
        o_b = attend_all(q, k, v)
    gates = jax.nn.sigmoid(z_g + p['b_gate'])
    merged = gates[..., :D_MODEL] * (o_a @ p['w_branch_a']) + gates[..., D_MODEL:] * (o_b @ p['w_branch_b'])
    x = x + gt2 * (merged @ p['w_out'])
    x = x + 0.5 * gt3 * swiglu(modulate(x, p['g_ffn2'], sh3, sc3), p['w_ffn2_up'], p['w_ffn2_down'])
    return x, ckv, krope, v_a


def setup_inputs(seed: int = 0) -> dict:
    key = jax.random.key(seed)
    ks = jax.random.split(key, 32)

    def nrm(i, shape, scale):
        return jax.random.normal(ks[i], shape, jnp.float32) * scale

    def w(i, fan_in, fan_out, mult=1.0):
        return nrm(i, (DEPTH, fan_in, fan_out), mult * fan_in ** -0.5)

    def gain(i, n):
        return 1.0 + nrm(i, (DEPTH, n), 0.02)

    return {
        'x_prompt': nrm(0, (BATCH, SEQ, D_MODEL), 1.0),
        'x_sample': nrm(1, (DEC_BATCH, DEC_SEQ, D_MODEL), 1.0),
        'c_prompt': nrm(2, (BATCH, D_MODEL), 1.0),
        'c_sample': nrm(3, (DEC_BATCH, D_MODEL), 1.0),
        'cache_ckv': nrm(4, (DEPTH, DEC_BATCH, PAST_LEN, KV_RANK), 1.0),
        'cache_krope': nrm(5, (DEPTH, DEC_BATCH, PAST_LEN, QK_ROPE), 1.0),
        'w_mod': w(6, D_MODEL, N_MOD * D_MODEL, 0.5),
        'b_mod': nrm(7, (DEPTH, N_MOD * D_MODEL), 0.01),
        'g_ffn1': gain(8, D_MODEL),
        'w_ffn1_up': w(9, D_MODEL, 2 * D_FF),
        'w_ffn1_down': w(10, D_FF, D_MODEL),
        'g_mix': gain(11, D_MODEL),
        'w_in': w(12, D_MODEL, N_IN),
        'g_gmlp_v': gain(13, D_A),
        'gmlp_ws': nrm(14, (DEPTH, GMLP_GROUPS, GMLP_CHUNK, GMLP_CHUNK), GMLP_CHUNK ** -0.5),
        'gmlp_b': 1.0 + nrm(15, (DEPTH, GMLP_GROUPS, GMLP_CHUNK), 0.02),
        'g_q_lat': gain(16, Q_RANK),
        'w_uq': w(17, Q_RANK, MLA_HEADS * QK_HEAD),
        'g_kv_lat': gain(18, KV_RANK),
        'w_uk': w(19, KV_RANK, MLA_HEADS * QK_NOPE),
        'w_uv': w(20, KV_RANK, MLA_HEADS * V_HEAD),
        'g_qnorm': gain(21, QK_HEAD),
        'g_knorm': gain(22, QK_HEAD),
        'b_gate': nrm(23, (DEPTH, 2 * D_MODEL), 0.01),
        'w_branch_a': w(24, D_A, D_MODEL),
        'w_branch_b': w(25, D_B, D_MODEL),
        'w_out': w(26, D_MODEL, D_MODEL),
        'g_ffn2': gain(27, D_MODEL),
        'w_ffn2_up': w(28, D_MODEL, 2 * D_FF),
        'w_ffn2_down': w(29, D_FF, D_MODEL),
    }


def reference(x_prompt, x_sample, c_prompt, c_sample, cache_ckv, cache_krope,
              w_mod, b_mod, g_ffn1, w_ffn1_up, w_ffn1_down, g_mix, w_in, g_gmlp_v, gmlp_ws, gmlp_b,
              g_q_lat, w_uq, g_kv_lat, w_uk, w_uv, g_qnorm, g_knorm, b_gate, w_branch_a, w_branch_b,
              w_out, g_ffn2, w_ffn2_up, w_ffn2_down):
    t_p = x_prompt.shape[1]
    t_s = x_sample.shape[1]
    past = cache_ckv.shape[2]
    pos_p = jnp.arange(t_p, dtype=jnp.float32)
    pos_s = jnp.arange(t_s, dtype=jnp.float32) + jnp.float32(past)
    xp, xs = x_prompt, x_sample
    ckv_p_l, kr_p_l, ckv_s_l, kr_s_l, vg_s_l = [], [], [], [], []
    for l in range(DEPTH):
        lp = {
            'w_mod': w_mod[l], 'b_mod': b_mod[l], 'g_ffn1': g_ffn1[l], 'w_ffn1_up': w_ffn1_up[l],
            'w_ffn1_down': w_ffn1_down[l], 'g_mix': g_mix[l], 'w_in': w_in[l], 'g_gmlp_v': g_gmlp_v[l],
            'gmlp_ws': gmlp_ws[l], 'gmlp_b': gmlp_b[l], 'g_q_lat': g_q_lat[l], 'w_uq': w_uq[l],
            'g_kv_lat': g_kv_lat[l], 'w_uk': w_uk[l], 'w_uv': w_uv[l], 'g_qnorm': g_qnorm[l],
            'g_knorm': g_knorm[l], 'b_gate': b_gate[l], 'w_branch_a': w_branch_a[l],
            'w_branch_b': w_branch_b[l], 'w_out': w_out[l], 'g_ffn2': g_ffn2[l],
            'w_ffn2_up': w_ffn2_up[l], 'w_ffn2_down': w_ffn2_down[l],
        }
        xp, ckv_p, kr_p, _ = layer(xp, c_prompt, pos_p, None, None, lp)
        xs, ckv_s, kr_s, vg_s = layer(xs, c_sample, pos_s, cache_ckv[l], cache_krope[l], lp)
        ckv_p_l.append(ckv_p)
        kr_p_l.append(kr_p)
        ckv_s_l.append(ckv_s)
        kr_s_l.append(kr_s)
        vg_s_l.append(vg_s)
    new_ckv_prompt = jnp.stack(ckv_p_l, axis=0)
    new_krope_prompt = jnp.stack(kr_p_l, axis=0)
    new_ckv_sample = jnp.stack(ckv_s_l, axis=0)
    new_krope_sample = jnp.stack(kr_s_l, axis=0)
    new_gmlp_v_sample = jnp.stack(vg_s_l, axis=0)
    return (xp, xs, new_ckv_prompt, new_krope_prompt, new_ckv_sample, new_krope_sample, new_gmlp_v_sample)
```

```python
import functools

import jax
import jax.numpy as jnp
from jax import lax
from jax.experimental import pallas as pl
from jax.experimental.pallas import tpu as pltpu

F32 = jnp.float32
BF16 = jnp.bfloat16

EPS = 1e-6
NEG = -1e30
ROPE_THETA = 10000.0
N_MOD = 9
CHUNK = 64
GMLP_CHUNK = 128
GMLP_GROUPS = 8
HEADS = 8
QK_NOPE = 64
QK_ROPE = 32
QK_HEAD = QK_NOPE + QK_ROPE
V_HEAD = 64
LANES = 128
VMEM_LIMIT = 60 * 1024 * 1024


def _in_offsets(d_a, q_rank, kv_rank, d_model):
    o_u = 0
    o_v = o_u + d_a
    o_q = o_v + d_a
    o_kv = o_q + q_rank
    o_kr = o_kv + kv_rank
    o_krs = o_kr + LANES
    o_g = o_krs + LANES
    o_end = o_g + 2 * d_model
    return o_u, o_v, o_q, o_kv, o_kr, o_krs, o_g, o_end


def _dot(a, b):
    return jnp.dot(a, b, preferred_element_type=F32)


def _dot_nt(a, b):
    return lax.dot_general(a, b, (((1,), (1,)), ((), ())), preferred_element_type=F32)


def _rms(x, g):
    ms = jnp.mean(x * x, axis=-1, keepdims=True)
    return x * lax.rsqrt(ms + EPS) * g


def _head_norm(x, g):
    ms = jnp.sum(x * x, axis=-1, keepdims=True) * (1.0 / QK_HEAD)
    return x * lax.rsqrt(ms + EPS) * g


def _mod_kernel(c_ref, w_ref, b_ref, o_ref):
    c = c_ref[...]
    a = (c * jax.nn.sigmoid(c)).astype(BF16)
    o_ref[...] = _dot(a, w_ref[...].astype(BF16)) + b_ref[...]


def _modulation(c, w_mod, b_mod):
    n, d = c.shape
    nout = w_mod.shape[1]
    bn = nout // 8
    return pl.pallas_call(
        _mod_kernel,
        grid=(nout // bn,),
        in_specs=[pl.BlockSpec((n, d), lambda j: (0, 0)),
                  pl.BlockSpec((d, bn), lambda j: (0, j)),
                  pl.BlockSpec((1, bn), lambda j: (0, j))],
        out_specs=pl.BlockSpec((n, bn), lambda j: (0, j)),
        out_shape=jax.ShapeDtypeStruct((n, nout), F32),
        compiler_params=pltpu.CompilerParams(dimension_semantics=("arbitrary",),
                                             vmem_limit_bytes=VMEM_LIMIT),
    )(c, w_mod, b_mod.reshape(1, nout))


def _ffn_kernel(x_ref, m_ref, g_ref, wup_ref, wdn_ref, o_ref, *, d_ff, chunks):
    x = x_ref[...]
    shift, scale, gate = m_ref[0], m_ref[1], m_ref[2]
    h = (_rms(x, g_ref[...]) * (1.0 + scale) + shift).astype(BF16)
    acc = None
    for c0, cw in chunks:
        a = _dot(h, wup_ref[:, c0:c0 + cw])
        b = _dot(h, wup_ref[:, d_ff + c0:d_ff + c0 + cw])
        act = (a * jax.nn.sigmoid(a) * b).astype(BF16)
        part = _dot(act, wdn_ref[c0:c0 + cw, :])
        acc = part if acc is None else acc + part
    o_ref[...] = x + (0.5 * gate) * acc


def _ffn_chunks(d_ff):
    step = 1024
    return tuple((c0, min(step, d_ff - c0)) for c0 in range(0, d_ff, step))


def _const_spec(shape):
    nd = len(shape)
    return pl.BlockSpec(shape, lambda *_: (0,) * nd, pipeline_mode=pl.Buffered(1))


def _ffn(x, m_spec, m_arr, g, w_up, w_down, tm):
    rows, d = x.shape
    d_ff = w_down.shape[0]
    kern = functools.partial(_ffn_kernel, d_ff=d_ff, chunks=_ffn_chunks(d_ff))
    return pl.pallas_call(
        kern,
        grid=(rows // tm,),
        in_specs=[pl.BlockSpec((tm, d), lambda i: (i, 0)),
                  m_spec,
                  _const_spec((1, d)),
                  _const_spec(w_up.shape),
                  _const_spec(w_down.shape)],
        out_specs=pl.BlockSpec((tm, d), lambda i: (i, 0)),
        out_shape=jax.ShapeDtypeStruct((rows, d), F32),
        compiler_params=pltpu.CompilerParams(dimension_semantics=("arbitrary",),
                                             vmem_limit_bytes=VMEM_LIMIT),
    )(x, m_arr, g.reshape(1, d), w_up, w_down)


def _tokenwise(x, m_ref, cos_ref, sin_ref, gmix_ref, win_ref, gv_ref, ws_ref, gb_ref, gql_ref,
               wuq_ref, gkv_ref, wuk_ref, wuv_ref, gqn_ref, gkn_ref, bg_ref, wa_ref, oa_scr,
               *, offs, chunk_len):
    o_u, o_v, o_q, o_kv, o_kr, o_krs, o_g, o_end = offs
    tm, d = x.shape
    shift, scale = m_ref[0], m_ref[1]
    h = (_rms(x, gmix_ref[...]) * (1.0 + scale) + shift).astype(BF16)

    u = jax.nn.gelu(_dot(h, win_ref[:, o_u:o_v]))
    v = _rms(jax.nn.gelu(_dot(h, win_ref[:, o_v:o_q])), gv_ref[...])
    vb = v.astype(BF16)
    L = chunk_len
    gw = d // GMLP_GROUPS
    row = lax.broadcasted_iota(jnp.int32, (L, L), 0)
    col = lax.broadcasted_iota(jnp.int32, (L, L), 1)
    tril = col <= row
    bias = gb_ref[...]
    for g in range(GMLP_GROUPS):
        wg = jnp.where(tril, ws_ref[g, 0:L, 0:L], 0.0).astype(BF16)
        for c in range(tm // L):
            r0 = c * L
            mixed = _dot(wg, vb[r0:r0 + L, g * gw:(g + 1) * gw]) + bias[:, g * gw:(g + 1) * gw]
            oa_scr[r0:r0 + L, g * gw:(g + 1) * gw] = (
                u[r0:r0 + L, g * gw:(g + 1) * gw] * mixed).astype(BF16)
    branch_a = _dot(oa_scr[...], wa_ref[...])

    gates = jax.nn.sigmoid(_dot(h, win_ref[:, o_g:o_end]) + bg_ref[...])
    merged_a = gates[:, :d] * branch_a
    gate_b = gates[:, d:]

    cos = cos_ref[...]
    sin = sin_ref[...]
    qn = _rms(_dot(h, win_ref[:, o_q:o_kv]), gql_ref[...]).astype(BF16)
    ckv = _rms(_dot(h, win_ref[:, o_kv:o_kr]), gkv_ref[...])
    kr2 = _dot(h, win_ref[:, o_kr:o_g])
    kr_slot = kr2[:, :LANES] * cos + kr2[:, LANES:] * sin
    ckvb = ckv.astype(BF16)
    q2 = _dot(qn, wuq_ref[...])
    kn = _dot(ckvb, wuk_ref[...])
    vv = _dot(ckvb, wuv_ref[...])
    gq = gqn_ref[...] * (QK_HEAD ** -0.5)
    gk = gkn_ref[...]
    hw = HEADS * LANES
    q_heads, k_heads = [], []
    for hd in range(HEADS):
        sl = slice(hd * LANES, (hd + 1) * LANES)
        qh = q2[:, sl] * cos + q2[:, hw + hd * LANES:hw + (hd + 1) * LANES] * sin
        q_heads.append(_head_norm(qh, gq).astype(BF16))
        k_heads.append(_head_norm(kn[:, sl] + kr_slot, gk).astype(BF16))
    v_pairs = [vv[:, p * LANES:(p + 1) * LANES].astype(BF16) for p in range(HEADS // 2)]
    return merged_a, gate_b, q_heads, k_heads, v_pairs, ckv, kr_slot, v


def _finish(x, gate, merged_a, gate_b, o_heads, wb_ref, wo_ref):
    tm = x.shape[0]
    lane = lax.broadcasted_iota(jnp.int32, (tm, LANES), 1)
    parts = [jnp.where(lane < V_HEAD, o_heads[2 * p], o_heads[2 * p + 1]).astype(BF16)
             for p in range(HEADS // 2)]
    o_b = jnp.concatenate(parts, axis=1)
    merged = merged_a + gate_b * _dot(o_b, wb_ref[...])
    return x + gate * _dot(merged.astype(BF16), wo_ref[...])


def _prompt_mixer_kernel(x_ref, m_ref, cos_ref, sin_ref, gmix_ref, win_ref, gv_ref, ws_ref, gb_ref,
                         gql_ref, wuq_ref, gkv_ref, wuk_ref, wuv_ref, gqn_ref, gkn_ref, bg_ref,
                         wa_ref, wb_ref, wo_ref,
                         y_ref, ckv_ref, kr_ref,
                         oa_scr, q_scr, k_scr, v_scr, o_scr, *, offs):
    t = pl.program_id(1)
    x = x_ref[...]
    tm = x.shape[0]
    merged_a, gate_b, q_heads, k_heads, v_pairs, ckv, kr_slot, _ = _tokenwise(
        x, m_ref, cos_ref, sin_ref, gmix_ref, win_ref, gv_ref, ws_ref, gb_ref, gql_ref, wuq_ref,
        gkv_ref, wuk_ref, wuv_ref, gqn_ref, gkn_ref, bg_ref, wa_ref, oa_scr,
        offs=offs, chunk_len=GMLP_CHUNK)
    ckv_ref[...] = ckv
    kr_ref[...] = kr_slot[:, QK_NOPE:QK_HEAD]
    row0 = pl.multiple_of(t * tm, tm)
    for hd in range(HEADS):
        q_scr[hd] = q_heads[hd]
        k_scr[hd, pl.ds(row0, tm), :] = k_heads[hd]
    for p in range(HEADS // 2):
        v_scr[p, pl.ds(row0, tm), :] = v_pairs[p]

    qrow = lax.broadcasted_iota(jnp.int32, (tm, tm), 0) // CHUNK
    kcol = lax.broadcasted_iota(jnp.int32, (tm, tm), 1) // CHUNK
    diag_mask = kcol <= qrow

    def head_body(hd, carry):
        q = q_scr[hd]
        pr = hd // 2
        s = _dot_nt(q, k_scr[hd, pl.ds(row0, tm), :])
        s = jnp.where(diag_mask, s, NEG)
        m0 = jnp.max(s, axis=1, keepdims=True)
        p0 = jnp.exp(s - m0)
        l0 = jnp.sum(p0, axis=1, keepdims=True)
        a0 = _dot(p0.astype(BF16), v_scr[pr, pl.ds(row0, tm), :])

        def kv_body(j, c):
            m, l, acc = c
            r = pl.multiple_of(j * tm, tm)
            sj = _dot_nt(q, k_scr[hd, pl.ds(r, tm), :])
            m_new = jnp.maximum(m, jnp.max(sj, axis=1, keepdims=True))
            alpha = jnp.exp(m - m_new)
            pj = jnp.exp(sj - m_new)
            l = alpha * l + jnp.sum(pj, axis=1, keepdims=True)
            acc = alpha * acc + _dot(pj.astype(BF16), v_scr[pr, pl.ds(r, tm), :])
            return m_new, l, acc

        _, l, acc = lax.fori_loop(0, t, kv_body, (m0, l0, a0))
        o_scr[hd] = acc / l
        return carry

    lax.fori_loop(0, HEADS, head_body, 0)
    o_heads = [o_scr[hd] for hd in range(HEADS)]
    y_ref[...] = _finish(x, m_ref[2], merged_a, gate_b, o_heads, wb_ref, wo_ref)


def _sample_mixer_kernel(x_ref, m_ref, cos_ref, sin_ref, pckv_ref, pkr_ref, gmix_ref, win_ref, gv_ref,
                         ws_ref, gb_ref, gql_ref, wuq_ref, gkv_ref, wuk_ref, wuv_ref, gqn_ref,
                         gkn_ref, bg_ref, wa_ref, wb_ref, wo_ref,
                         y_ref, ckv_ref, kr_ref, gv_out_ref,
                         oa_scr, k_scr, v_scr, *, offs, key_chunk):
    x = x_ref[...]
    tm = x.shape[0]
    past = pckv_ref.shape[0]
    merged_a, gate_b, q_heads, k_heads, v_pairs, ckv, kr_slot, v_gmlp = _tokenwise(
        x, m_ref, cos_ref, sin_ref, gmix_ref, win_ref, gv_ref, ws_ref, gb_ref, gql_ref, wuq_ref,
        gkv_ref, wuk_ref, wuv_ref, gqn_ref, gkn_ref, bg_ref, wa_ref, oa_scr,
        offs=offs, chunk_len=min(tm, GMLP_CHUNK))
    ckv_ref[...] = ckv
    kr_ref[...] = kr_slot[:, QK_NOPE:QK_HEAD]
    gv_out_ref[...] = v_gmlp

    gk = gkn_ref[...]

    def build(i, carry):
        r = pl.multiple_of(i * key_chunk, key_chunk)
        cb = pckv_ref[pl.ds(r, key_chunk), :].astype(BF16)
        kn = _dot(cb, wuk_ref[...])
        vv = _dot(cb, wuv_ref[...])
        kr = pkr_ref[pl.ds(r, key_chunk), :]
        for hd in range(HEADS):
            kh = kn[:, hd * LANES:(hd + 1) * LANES] + kr
            k_scr[hd, pl.ds(r, key_chunk), :] = _head_norm(kh, gk).astype(BF16)
        for p in range(HEADS // 2):
            v_scr[p, pl.ds(r, key_chunk), :] = vv[:, p * LANES:(p + 1) * LANES].astype(BF16)
        return carry

    lax.fori_loop(0, past // key_chunk, build, 0)

    o_heads = []
    for hd in range(HEADS):
        q = q_heads[hd]
        sp = _dot_nt(q, k_scr[hd])
        sn = _dot_nt(q, k_heads[hd])
        m = jnp.maximum(jnp.max(sp, axis=1, keepdims=True), jnp.max(sn, axis=1, keepdims=True))
        pp = jnp.exp(sp - m)
        pn = jnp.exp(sn - m)
        l = jnp.sum(pp, axis=1, keepdims=True) + jnp.sum(pn, axis=1, keepdims=True)
        acc = _dot(pp.astype(BF16), v_scr[hd // 2]) + _dot(pn.astype(BF16), v_pairs[hd // 2])
        o_heads.append(acc / l)
    y_ref[...] = _finish(x, m_ref[2], merged_a, gate_b, o_heads, wb_ref, wo_ref)


def _mixer_weight_specs(wts):
    return [_const_spec(w.shape) for w in wts]


def _prompt_mixer(x, m_arr, cos, sin, wts, offs, batch, seq, tm, kv_rank):
    rows, d = x.shape
    nt = seq // tm
    kern = functools.partial(_prompt_mixer_kernel, offs=offs)
    row_spec = lambda w: pl.BlockSpec((tm, w), lambda b, t: (b * nt + t, 0))
    return pl.pallas_call(
        kern,
        grid=(batch, nt),
        in_specs=[row_spec(d),
                  pl.BlockSpec((None, 3, None, 1, d), lambda b, t: (1, 0, b, 0, 0)),
                  pl.BlockSpec((tm, LANES), lambda b, t: (t, 0)),
                  pl.BlockSpec((tm, LANES), lambda b, t: (t, 0))] + _mixer_weight_specs(wts),
        out_specs=[row_spec(d), row_spec(kv_rank), row_spec(QK_ROPE)],
        out_shape=[jax.ShapeDtypeStruct((rows, d), F32),
                   jax.ShapeDtypeStruct((rows, kv_rank), F32),
                   jax.ShapeDtypeStruct((rows, QK_ROPE), F32)],
        scratch_shapes=[pltpu.VMEM((tm, d), BF16),
                        pltpu.VMEM((HEADS, tm, LANES), BF16),
                        pltpu.VMEM((HEADS, seq, LANES), BF16),
                        pltpu.VMEM((HEADS // 2, seq, LANES), BF16),
                        pltpu.VMEM((HEADS, tm, LANES), F32)],
        compiler_params=pltpu.CompilerParams(dimension_semantics=("arbitrary", "arbitrary"),
                                             vmem_limit_bytes=VMEM_LIMIT),
    )(x, m_arr, cos, sin, *wts)


def _sample_mixer(x, m_arr, cos, sin, past_ckv, past_kr_slot, wts, offs, batch, seq, kv_rank):
    rows, d = x.shape
    past = past_ckv.shape[1]
    key_chunk = 512 if past % 512 == 0 else past
    kern = functools.partial(_sample_mixer_kernel, offs=offs, key_chunk=key_chunk)
    row_spec = lambda w: pl.BlockSpec((seq, w), lambda b: (b, 0))
    return pl.pallas_call(
        kern,
        grid=(batch,),
        in_specs=[row_spec(d),
                  pl.BlockSpec((None, 3, None, 1, d), lambda b: (1, 0, b, 0, 0)),
                  pl.BlockSpec((seq, LANES), lambda b: (0, 0)),
                  pl.BlockSpec((seq, LANES), lambda b: (0, 0)),
                  pl.BlockSpec((None, past, kv_rank), lambda b: (b, 0, 0)),
                  pl.BlockSpec((None, past, LANES), lambda b: (b, 0, 0))] + _mixer_weight_specs(wts),
        out_specs=[row_spec(d), row_spec(kv_rank), row_spec(QK_ROPE), row_spec(d)],
        out_shape=[jax.ShapeDtypeStruct((rows, d), F32),
                   jax.ShapeDtypeStruct((rows, kv_rank), F32),
                   jax.ShapeDtypeStruct((rows, QK_ROPE), F32),
                   jax.ShapeDtypeStruct((rows, d), F32)],
        scratch_shapes=[pltpu.VMEM((seq, d), BF16),
                        pltpu.VMEM((HEADS, past, LANES), BF16),
                        pltpu.VMEM((HEADS // 2, past, LANES), BF16)],
        compiler_params=pltpu.CompilerParams(dimension_semantics=("arbitrary",),
                                             vmem_limit_bytes=VMEM_LIMIT),
    )(x, m_arr, cos, sin, past_ckv, past_kr_slot, *wts)


def _rope_tables(pos):
    half = QK_ROPE // 2
    freqs = ROPE_THETA ** (-jnp.arange(half, dtype=F32) / half)
    ang = pos[:, None] * freqs[None, :]
    cos, sin = jnp.cos(ang), jnp.sin(ang)
    n = pos.shape[0]
    pad = LANES - QK_HEAD
    c = jnp.concatenate([jnp.ones((n, QK_NOPE), F32), cos, cos, jnp.zeros((n, pad), F32)], axis=1)
    s = jnp.concatenate([jnp.zeros((n, QK_NOPE), F32), -sin, sin, jnp.zeros((n, pad), F32)], axis=1)
    return c, s


def _rope_slot(w):
    half = QK_ROPE // 2
    lead = w.shape[:-1]
    z0 = jnp.zeros(lead + (QK_NOPE,), w.dtype)
    z1 = jnp.zeros(lead + (LANES - QK_HEAD,), w.dtype)
    slot = jnp.concatenate([z0, w, z1], axis=-1)
    swapped = jnp.concatenate([z0, w[..., half:], w[..., :half], z1], axis=-1)
    return slot, swapped


def _layer_weights(g_mix, w_in, g_gmlp_v, gmlp_ws, gmlp_b, g_q_lat, w_uq, g_kv_lat, w_uk, w_uv,
                   g_qnorm, g_knorm, b_gate, w_branch_a, w_branch_b, w_out, chunk_len):
    d = w_in.shape[0]
    d_a = g_gmlp_v.shape[0]
    q_rank = g_q_lat.shape[0]
    kv_rank = g_kv_lat.shape[0]
    o = 0
    w_u = w_in[:, o:o + d_a]; o += d_a
    w_v = w_in[:, o:o + d_a]; o += d_a
    w_q = w_in[:, o:o + q_rank]; o += q_rank
    w_kv = w_in[:, o:o + kv_rank]; o += kv_rank
    w_kr = w_in[:, o:o + QK_ROPE]; o += QK_ROPE
    w_g = w_in[:, o:]
    kr_slot, kr_swapped = _rope_slot(w_kr)
    win_p = jnp.concatenate([w_u, w_v, w_q, w_kv, kr_slot, kr_swapped, w_g], axis=1).astype(BF16)

    uq3 = w_uq.reshape(q_rank, HEADS, QK_HEAD)
    rope_slot, rope_swapped = _rope_slot(uq3[..., QK_NOPE:])
    nope = jnp.concatenate([uq3[..., :QK_NOPE], jnp.zeros((q_rank, HEADS, LANES - QK_NOPE), F32)], -1)
    wuq_p = jnp.concatenate([(nope + rope_slot).reshape(q_rank, HEADS * LANES),
                             rope_swapped.reshape(q_rank, HEADS * LANES)], axis=1).astype(BF16)
    uk3 = w_uk.reshape(kv_rank, HEADS, QK_NOPE)
    wuk_p = jnp.concatenate([uk3, jnp.zeros((kv_rank, HEADS, LANES - QK_NOPE), F32)], -1)
    wuk_p = wuk_p.reshape(kv_rank, HEADS * LANES).astype(BF16)
    pad = jnp.zeros((LANES - QK_HEAD,), F32)
    gq = jnp.concatenate([g_qnorm, pad]).reshape(1, LANES)
    gk = jnp.concatenate([g_knorm, pad]).reshape(1, LANES)
    gw = d_a // GMLP_GROUPS
    bias = jnp.repeat(gmlp_b[:, :chunk_len].T, gw, axis=1)
    wts = [g_mix.reshape(1, d), win_p, g_gmlp_v.reshape(1, d_a), gmlp_ws, bias,
           g_q_lat.reshape(1, q_rank), wuq_p, g_kv_lat.reshape(1, kv_rank), wuk_p, w_uv.astype(BF16),
           gq, gk, b_gate.reshape(1, -1), w_branch_a.astype(BF16), w_branch_b.astype(BF16),
           w_out.astype(BF16)]
    offs = _in_offsets(d_a, q_rank, kv_rank, d)
    return wts, offs


def _prompt_tile(seq):
    for tm in (512, 256, 128):
        if seq % tm == 0:
            return tm
    raise ValueError("prompt length must be a multiple of 128")


def kernel(x_prompt, x_sample, c_prompt, c_sample, cache_ckv, cache_krope, w_mod, b_mod, g_ffn1, w_ffn1_up, w_ffn1_down, g_mix, w_in, g_gmlp_v, gmlp_ws, gmlp_b, g_q_lat, w_uq, g_kv_lat, w_uk, w_uv, g_qnorm, g_knorm, b_gate, w_branch_a, w_branch_b, w_out, g_ffn2, w_ffn2_up, w_ffn2_down):
    bp, tp, d = x_prompt.shape
    bs, ts, _ = x_sample.shape
    depth = w_mod.shape[0]
    past = cache_ckv.shape[2]
    kv_rank = g_kv_lat.shape[1]
    tm = _prompt_tile(tp)
    assert ts % 8 == 0 and ts <= GMLP_CHUNK and (bs * ts) % 8 == 0

    cos_p, sin_p = _rope_tables(jnp.arange(tp, dtype=F32))
    cos_s, sin_s = _rope_tables(jnp.arange(ts, dtype=F32) + jnp.float32(past))
    c_all = jnp.concatenate([c_prompt, c_sample], axis=0)

    xp = x_prompt.reshape(bp * tp, d)
    xs = x_sample.reshape(bs * ts, d)
    outs = {k: [] for k in ("ckv_p", "kr_p", "ckv_s", "kr_s", "vg_s")}
    for l in range(depth):
        m = _modulation(c_all, w_mod[l], b_mod[l])
        m5 = m.reshape(bp + bs, 3, 3, d).transpose(1, 2, 0, 3)
        m_p = m5[:, :, :bp].reshape(3, 3, bp, 1, d)
        m_s = m5[:, :, bp:].reshape(3, 3, bs, 1, d)
        m_s_rows = jnp.repeat(m5[:, :, bp:], ts, axis=2)
        nt = tp // tm
        rows_s = bs * ts

        def ffn_pair(sub, g, w_up, w_down, xp, xs):
            w_up_b, w_down_b = w_up.astype(BF16), w_down.astype(BF16)
            p_spec = pl.BlockSpec((None, 3, None, 1, d), lambda i: (sub, 0, i // nt, 0, 0))
            s_spec = pl.BlockSpec((None, 3, rows_s, d), lambda i: (sub, 0, 0, 0))
            return (_ffn(xp, p_spec, m_p, g, w_up_b, w_down_b, tm),
                    _ffn(xs, s_spec, m_s_rows, g, w_up_b, w_down_b, rows_s))

        xp, xs = ffn_pair(0, g_ffn1[l], w_ffn1_up[l], w_ffn1_down[l], xp, xs)

        layer_args = (g_mix[l], w_in[l], g_gmlp_v[l], gmlp_ws[l], gmlp_b[l], g_q_lat[l], w_uq[l],
                      g_kv_lat[l], w_uk[l], w_uv[l], g_qnorm[l], g_knorm[l], b_gate[l],
                      w_branch_a[l], w_branch_b[l], w_out[l])
        wts_p, offs = _layer_weights(*layer_args, chunk_len=GMLP_CHUNK)
        xp, ckv_p, kr_p = _prompt_mixer(xp, m_p, cos_p, sin_p, wts_p, offs, bp, tp, tm, kv_rank)
        wts_s, _ = _layer_weights(*layer_args, chunk_len=min(ts, GMLP_CHUNK))
        past_kr_slot, _ = _rope_slot(cache_krope[l])
        xs, ckv_s, kr_s, vg_s = _sample_mixer(xs, m_s, cos_s, sin_s, cache_ckv[l], past_kr_slot,
                                              wts_s, offs, bs, ts, kv_rank)

        xp, xs = ffn_pair(2, g_ffn2[l], w_ffn2_up[l], w_ffn2_down[l], xp, xs)

        outs["ckv_p"].append(ckv_p.reshape(bp, tp, kv_rank))
        outs["kr_p"].append(kr_p.reshape(bp, tp, QK_ROPE))
        outs["ckv_s"].append(ckv_s.reshape(bs, ts, kv_rank))
        outs["kr_s"].append(kr_s.reshape(bs, ts, QK_ROPE))
        outs["vg_s"].append(vg_s.reshape(bs, ts, d))
    return (xp.reshape(bp, tp, d), xs.reshape(bs, ts, d),
            jnp.stack(outs["ckv_p"], 0), jnp.stack(outs["kr_p"], 0),
            jnp.stack(outs["ckv_s"], 0), jnp.stack(outs["kr_s"], 0), jnp.stack(outs["vg_s"], 0))
```

```python
import functools

import jax
import jax.numpy as jnp
from jax import lax
from jax.experimental import pallas as pl
from jax.experimental.pallas import tpu as pltpu

F32 = jnp.float32
BF16 = jnp.bfloat16

EPS = 1e-6
NEG = -1e30
ROPE_THETA = 10000.0
N_MOD = 9
CHUNK = 64
GMLP_CHUNK = 128
GMLP_GROUPS = 8
HEADS = 8
QK_NOPE = 64
QK_ROPE = 32
QK_HEAD = QK_NOPE + QK_ROPE
V_HEAD = 64
LANES = 128
VMEM_LIMIT = 60 * 1024 * 1024


def _in_offsets(d_a, q_rank, kv_rank, d_model):
    o_u = 0
    o_v = o_u + d_a
    o_q = o_v + d_a
    o_kv = o_q + q_rank
    o_kr = o_kv + kv_rank
    o_krs = o_kr + LANES
    o_g = o_krs + LANES
    o_end = o_g + 2 * d_model
    return o_u, o_v, o_q, o_kv, o_kr, o_krs, o_g, o_end


def _dot(a, b):
    return jnp.dot(a, b, preferred_element_type=F32)


def _dot_nt(a, b):
    return lax.dot_general(a, b, (((1,), (1,)), ((), ())), preferred_element_type=F32)


def _rms(x, g):
    ms = jnp.mean(x * x, axis=-1, keepdims=True)
    return x * lax.rsqrt(ms + EPS) * g


def _head_norm(x, g):
    ms = jnp.sum(x * x, axis=-1, keepdims=True) * (1.0 / QK_HEAD)
    return x * lax.rsqrt(ms + EPS) * g


def _mod_kernel(c_ref, w_ref, b_ref, o_ref):
    c = c_ref[...]
    a = (c * jax.nn.sigmoid(c)).astype(BF16)
    o_ref[...] = _dot(a, w_ref[...].astype(BF16)) + b_ref[...]


def _modulation(c, w_mod, b_mod):
    n, d = c.shape
    nout = w_mod.shape[1]
    bn = nout // 8
    return pl.pallas_call(
        _mod_kernel,
        grid=(nout // bn,),
        in_specs=[pl.BlockSpec((n, d), lambda j: (0, 0)),
                  pl.BlockSpec((d, bn), lambda j: (0, j)),
                  pl.BlockSpec((1, bn), lambda j: (0, j))],
        out_specs=pl.BlockSpec((n, bn), lambda j: (0, j)),
        out_shape=jax.ShapeDtypeStruct((n, nout), F32),
        compiler_params=pltpu.CompilerParams(dimension_semantics=("arbitrary",),
                                             vmem_limit_bytes=VMEM_LIMIT),
    )(c, w_mod, b_mod.reshape(1, nout))


def _ffn_kernel(x_ref, m_ref, g_ref, wup_ref, wdn_ref, o_ref, *, d_ff, chunks):
    x = x_ref[...]
    shift, scale, gate = m_ref[0], m_ref[1], m_ref[2]
    h = (_rms(x, g_ref[...]) * (1.0 + scale) + shift).astype(BF16)
    acc = None
    for c0, cw in chunks:
        a = _dot(h, wup_ref[:, c0:c0 + cw])
        b = _dot(h, wup_ref[:, d_ff + c0:d_ff + c0 + cw])
        act = (a * jax.nn.sigmoid(a) * b).astype(BF16)
        part = _dot(act, wdn_ref[c0:c0 + cw, :])
        acc = part if acc is None else acc + part
    o_ref[...] = x + (0.5 * gate) * acc


def _ffn_chunks(d_ff):
    step = 1024
    return tuple((c0, min(step, d_ff - c0)) for c0 in range(0, d_ff, step))


def _const_spec(shape):
    nd = len(shape)
    return pl.BlockSpec(shape, lambda *_: (0,) * nd, pipeline_mode=pl.Buffered(1))


def _ffn(x, m_spec, m_arr, g, w_up, w_down, tm):
    rows, d = x.shape
    d_ff = w_down.shape[0]
    kern = functools.partial(_ffn_kernel, d_ff=d_ff, chunks=_ffn_chunks(d_ff))
    return pl.pallas_call(
        kern,
        grid=(rows // tm,),
        in_specs=[pl.BlockSpec((tm, d), lambda i: (i, 0)),
                  m_spec,
                  _const_spec((1, d)),
                  _const_spec(w_up.shape),
                  _const_spec(w_down.shape)],
        out_specs=pl.BlockSpec((tm, d), lambda i: (i, 0)),
        out_shape=jax.ShapeDtypeStruct((rows, d), F32),
        compiler_params=pltpu.CompilerParams(dimension_semantics=("arbitrary",),
                                             vmem_limit_bytes=VMEM_LIMIT),
    )(x, m_arr, g.reshape(1, d), w_up, w_down)


def _tokenwise(x, m_ref, cos_ref, sin_ref, gmix_ref, win_ref, gv_ref, ws_ref, gb_ref, gql_ref,
               wuq_ref, gkv_ref, wuk_ref, wuv_ref, gqn_ref, gkn_ref, bg_ref, wa_ref, oa_scr,
               *, offs, chunk_len):
    o_u, o_v, o_q, o_kv, o_kr, o_krs, o_g, o_end = offs
    tm, d = x.shape
    shift, scale = m_ref[0], m_ref[1]
    h = (_rms(x, gmix_ref[...]) * (1.0 + scale) + shift).astype(BF16)

    u = jax.nn.gelu(_dot(h, win_ref[:, o_u:o_v]))
    v = _rms(jax.nn.gelu(_dot(h, win_ref[:, o_v:o_q])), gv_ref[...])
    vb = v.astype(BF16)
    L = chunk_len
    gw = d // GMLP_GROUPS
    row = lax.broadcasted_iota(jnp.int32, (L, L), 0)
    col = lax.broadcasted_iota(jnp.int32, (L, L), 1)
    tril = col <= row
    bias = gb_ref[...]
    for g in range(GMLP_GROUPS):
        wg = jnp.where(tril, ws_ref[g, 0:L, 0:L], 0.0).astype(BF16)
        for c in range(tm // L):
            r0 = c * L
            mixed = _dot(wg, vb[r0:r0 + L, g * gw:(g + 1) * gw]) + bias[:, g * gw:(g + 1) * gw]
            oa_scr[r0:r0 + L, g * gw:(g + 1) * gw] = (
                u[r0:r0 + L, g * gw:(g + 1) * gw] * mixed).astype(BF16)
    branch_a = _dot(oa_scr[...], wa_ref[...])

    gates = jax.nn.sigmoid(_dot(h, win_ref[:, o_g:o_end]) + bg_ref[...])
    merged_a = gates[:, :d] * branch_a
    gate_b = gates[:, d:]

    cos = cos_ref[...]
    sin = sin_ref[...]
    qn = _rms(_dot(h, win_ref[:, o_q:o_kv]), gql_ref[...]).astype(BF16)
    ckv = _rms(_dot(h, win_ref[:, o_kv:o_kr]), gkv_ref[...])
    kr2 = _dot(h, win_ref[:, o_kr:o_g])
    kr_slot = kr2[:, :LANES] * cos + kr2[:, LANES:] * sin
    ckvb = ckv.astype(BF16)
    q2 = _dot(qn, wuq_ref[...])
    kn = _dot(ckvb, wuk_ref[...])
    vv = _dot(ckvb, wuv_ref[...])
    gq = gqn_ref[...] * (QK_HEAD ** -0.5)
    gk = gkn_ref[...]
    hw = HEADS * LANES
    q_heads, k_heads = [], []
    for hd in range(HEADS):
        sl = slice(hd * LANES, (hd + 1) * LANES)
        qh = q2[:, sl] * cos + q2[:, hw + hd * LANES:hw + (hd + 1) * LANES] * sin
        q_heads.append(_head_norm(qh, gq).astype(BF16))
        k_heads.append(_head_norm(kn[:, sl] + kr_slot, gk).astype(BF16))
    v_heads = _v_slots(vv)
    return merged_a, gate_b, q_heads, k_heads, v_heads, ckv, kr_slot, v


def _v_slots(vv):
    one = (lax.broadcasted_iota(jnp.int32, (1, LANES), 1) == V_HEAD).astype(F32)
    return [(vv[:, hd * LANES:(hd + 1) * LANES] + one).astype(BF16) for hd in range(HEADS)]


def _normalise(acc):
    return acc / acc[:, V_HEAD:V_HEAD + 1]


def _finish(x, gate, merged_a, gate_b, o_heads, wb_ref, wo_ref):
    tm = x.shape[0]
    lane = lax.broadcasted_iota(jnp.int32, (tm, LANES), 1)
    parts = [jnp.where(lane < V_HEAD, o_heads[2 * p],
                       pltpu.roll(o_heads[2 * p + 1], V_HEAD, axis=1)).astype(BF16)
             for p in range(HEADS // 2)]
    o_b = jnp.concatenate(parts, axis=1)
    merged = merged_a + gate_b * _dot(o_b, wb_ref[...])
    return x + gate * _dot(merged.astype(BF16), wo_ref[...])


def _prompt_mixer_kernel(x_ref, m_ref, cos_ref, sin_ref, gmix_ref, win_ref, gv_ref, ws_ref, gb_ref,
                         gql_ref, wuq_ref, gkv_ref, wuk_ref, wuv_ref, gqn_ref, gkn_ref, bg_ref,
                         wa_ref, wb_ref, wo_ref,
                         y_ref, ckv_ref, kr_ref,
                         oa_scr, q_scr, k_scr, v_scr, *, offs):
    t = pl.program_id(1)
    x = x_ref[...]
    tm = x.shape[0]
    merged_a, gate_b, q_heads, k_heads, v_heads, ckv, kr_slot, _ = _tokenwise(
        x, m_ref, cos_ref, sin_ref, gmix_ref, win_ref, gv_ref, ws_ref, gb_ref, gql_ref, wuq_ref,
        gkv_ref, wuk_ref, wuv_ref, gqn_ref, gkn_ref, bg_ref, wa_ref, oa_scr,
        offs=offs, chunk_len=GMLP_CHUNK)
    ckv_ref[...] = ckv
    kr_ref[...] = kr_slot[:, QK_NOPE:QK_HEAD]
    row0 = pl.multiple_of(t * tm, tm)
    for hd in range(HEADS):
        q_scr[hd] = q_heads[hd]
        k_scr[hd, pl.ds(row0, tm), :] = k_heads[hd]
        v_scr[hd, pl.ds(row0, tm), :] = v_heads[hd]

    qrow = lax.broadcasted_iota(jnp.int32, (tm, tm), 0) // CHUNK
    kcol = lax.broadcasted_iota(jnp.int32, (tm, tm), 1) // CHUNK
    diag_mask = kcol <= qrow
    ms, accs = [], []
    for hd in range(HEADS):
        s = jnp.where(diag_mask, _dot_nt(q_heads[hd], k_heads[hd]), NEG)
        m0 = jnp.max(s, axis=1, keepdims=True)
        ms.append(m0)
        accs.append(_dot(jnp.exp(s - m0).astype(BF16), v_heads[hd]))

    def kv_body(j, carry):
        ms, accs = carry
        r = pl.multiple_of(j * tm, tm)
        new_ms, new_accs = [], []
        for hd in range(HEADS):
            sj = _dot_nt(q_scr[hd], k_scr[hd, pl.ds(r, tm), :])
            m_new = jnp.maximum(ms[hd], jnp.max(sj, axis=1, keepdims=True))
            pj = jnp.exp(sj - m_new).astype(BF16)
            new_ms.append(m_new)
            new_accs.append(jnp.exp(ms[hd] - m_new) * accs[hd] + _dot(pj, v_scr[hd, pl.ds(r, tm), :]))
        return tuple(new_ms), tuple(new_accs)

    _, accs = lax.fori_loop(0, t, kv_body, (tuple(ms), tuple(accs)))
    o_heads = [_normalise(a) for a in accs]
    y_ref[...] = _finish(x, m_ref[2], merged_a, gate_b, o_heads, wb_ref, wo_ref)


def _sample_mixer_kernel(x_ref, m_ref, cos_ref, sin_ref, pckv_ref, pkr_ref, gmix_ref, win_ref, gv_ref,
                         ws_ref, gb_ref, gql_ref, wuq_ref, gkv_ref, wuk_ref, wuv_ref, gqn_ref,
                         gkn_ref, bg_ref, wa_ref, wb_ref, wo_ref,
                         y_ref, ckv_ref, kr_ref, gv_out_ref,
                         oa_scr, k_scr, v_scr, *, offs, key_chunk):
    x = x_ref[...]
    tm = x.shape[0]
    past = pckv_ref.shape[0]
    merged_a, gate_b, q_heads, k_heads, v_heads, ckv, kr_slot, v_gmlp = _tokenwise(
        x, m_ref, cos_ref, sin_ref, gmix_ref, win_ref, gv_ref, ws_ref, gb_ref, gql_ref, wuq_ref,
        gkv_ref, wuk_ref, wuv_ref, gqn_ref, gkn_ref, bg_ref, wa_ref, oa_scr,
        offs=offs, chunk_len=min(tm, GMLP_CHUNK))
    ckv_ref[...] = ckv
    kr_ref[...] = kr_slot[:, QK_NOPE:QK_HEAD]
    gv_out_ref[...] = v_gmlp

    gk = gkn_ref[...]

    def build(i, carry):
        r = pl.multiple_of(i * key_chunk, key_chunk)
        cb = pckv_ref[pl.ds(r, key_chunk), :].astype(BF16)
        kn = _dot(cb, wuk_ref[...])
        vs = _v_slots(_dot(cb, wuv_ref[...]))
        kr = pkr_ref[pl.ds(r, key_chunk), :]
        for hd in range(HEADS):
            kh = kn[:, hd * LANES:(hd + 1) * LANES] + kr
            k_scr[hd, pl.ds(r, key_chunk), :] = _head_norm(kh, gk).astype(BF16)
            v_scr[hd, pl.ds(r, key_chunk), :] = vs[hd]
        return carry

    lax.fori_loop(0, past // key_chunk, build, 0)

    o_heads = []
    for hd in range(HEADS):
        q = q_heads[hd]
        sp = _dot_nt(q, k_scr[hd])
        sn = _dot_nt(q, k_heads[hd])
        m = jnp.maximum(jnp.max(sp, axis=1, keepdims=True), jnp.max(sn, axis=1, keepdims=True))
        pp = jnp.exp(sp - m).astype(BF16)
        pn = jnp.exp(sn - m).astype(BF16)
        o_heads.append(_normalise(_dot(pp, v_scr[hd]) + _dot(pn, v_heads[hd])))
    y_ref[...] = _finish(x, m_ref[2], merged_a, gate_b, o_heads, wb_ref, wo_ref)


def _mixer_weight_specs(wts):
    return [_const_spec(w.shape) for w in wts]


def _prompt_mixer(x, m_arr, cos, sin, wts, offs, batch, seq, tm, kv_rank):
    rows, d = x.shape
    nt = seq // tm
    kern = functools.partial(_prompt_mixer_kernel, offs=offs)
    row_spec = lambda w: pl.BlockSpec((tm, w), lambda b, t: (b * nt + t, 0))
    return pl.pallas_call(
        kern,
        grid=(batch, nt),
        in_specs=[row_spec(d),
                  pl.BlockSpec((None, 3, None, 1, d), lambda b, t: (1, 0, b, 0, 0)),
                  pl.BlockSpec((tm, LANES), lambda b, t: (t, 0)),
                  pl.BlockSpec((tm, LANES), lambda b, t: (t, 0))] + _mixer_weight_specs(wts),
        out_specs=[row_spec(d), row_spec(kv_rank), row_spec(QK_ROPE)],
        out_shape=[jax.ShapeDtypeStruct((rows, d), F32),
                   jax.ShapeDtypeStruct((rows, kv_rank), F32),
                   jax.ShapeDtypeStruct((rows, QK_ROPE), F32)],
        scratch_shapes=[pltpu.VMEM((tm, d), BF16),
                        pltpu.VMEM((HEADS, tm, LANES), BF16),
                        pltpu.VMEM((HEADS, seq, LANES), BF16),
                        pltpu.VMEM((HEADS, seq, LANES), BF16)],
        compiler_params=pltpu.CompilerParams(dimension_semantics=("arbitrary", "arbitrary"),
                                             vmem_limit_bytes=VMEM_LIMIT),
    )(x, m_arr, cos, sin, *wts)


def _sample_mixer(x, m_arr, cos, sin, past_ckv, past_kr_slot, wts, offs, batch, seq, kv_rank):
    rows, d = x.shape
    past = past_ckv.shape[1]
    key_chunk = 512 if past % 512 == 0 else past
    kern = functools.partial(_sample_mixer_kernel, offs=offs, key_chunk=key_chunk)
    row_spec = lambda w: pl.BlockSpec((seq, w), lambda b: (b, 0))
    return pl.pallas_call(
        kern,
        grid=(batch,),
        in_specs=[row_spec(d),
                  pl.BlockSpec((None, 3, None, 1, d), lambda b: (1, 0, b, 0, 0)),
                  pl.BlockSpec((seq, LANES), lambda b: (0, 0)),
                  pl.BlockSpec((seq, LANES), lambda b: (0, 0)),
                  pl.BlockSpec((None, past, kv_rank), lambda b: (b, 0, 0)),
                  pl.BlockSpec((None, past, LANES), lambda b: (b, 0, 0))] + _mixer_weight_specs(wts),
        out_specs=[row_spec(d), row_spec(kv_rank), row_spec(QK_ROPE), row_spec(d)],
        out_shape=[jax.ShapeDtypeStruct((rows, d), F32),
                   jax.ShapeDtypeStruct((rows, kv_rank), F32),
                   jax.ShapeDtypeStruct((rows, QK_ROPE), F32),
                   jax.ShapeDtypeStruct((rows, d), F32)],
        scratch_shapes=[pltpu.VMEM((seq, d), BF16),
                        pltpu.VMEM((HEADS, past, LANES), BF16),
                        pltpu.VMEM((HEADS, past, LANES), BF16)],
        compiler_params=pltpu.CompilerParams(dimension_semantics=("arbitrary",),
                                             vmem_limit_bytes=VMEM_LIMIT),
    )(x, m_arr, cos, sin, past_ckv, past_kr_slot, *wts)


def _rope_tables(pos):
    half = QK_ROPE // 2
    freqs = ROPE_THETA ** (-jnp.arange(half, dtype=F32) / half)
    ang = pos[:, None] * freqs[None, :]
    cos, sin = jnp.cos(ang), jnp.sin(ang)
    n = pos.shape[0]
    pad = LANES - QK_HEAD
    c = jnp.concatenate([jnp.ones((n, QK_NOPE), F32), cos, cos, jnp.zeros((n, pad), F32)], axis=1)
    s = jnp.concatenate([jnp.zeros((n, QK_NOPE), F32), -sin, sin, jnp.zeros((n, pad), F32)], axis=1)
    return c, s


def _rope_slot(w):
    half = QK_ROPE // 2
    lead = w.shape[:-1]
    z0 = jnp.zeros(lead + (QK_NOPE,), w.dtype)
    z1 = jnp.zeros(lead + (LANES - QK_HEAD,), w.dtype)
    slot = jnp.concatenate([z0, w, z1], axis=-1)
    swapped = jnp.concatenate([z0, w[..., half:], w[..., :half], z1], axis=-1)
    return slot, swapped


def _layer_weights(g_mix, w_in, g_gmlp_v, gmlp_ws, gmlp_b, g_q_lat, w_uq, g_kv_lat, w_uk, w_uv,
                   g_qnorm, g_knorm, b_gate, w_branch_a, w_branch_b, w_out, chunk_len):
    d = w_in.shape[0]
    d_a = g_gmlp_v.shape[0]
    q_rank = g_q_lat.shape[0]
    kv_rank = g_kv_lat.shape[0]
    o = 0
    w_u = w_in[:, o:o + d_a]; o += d_a
    w_v = w_in[:, o:o + d_a]; o += d_a
    w_q = w_in[:, o:o + q_rank]; o += q_rank
    w_kv = w_in[:, o:o + kv_rank]; o += kv_rank
    w_kr = w_in[:, o:o + QK_ROPE]; o += QK_ROPE
    w_g = w_in[:, o:]
    kr_slot, kr_swapped = _rope_slot(w_kr)
    win_p = jnp.concatenate([w_u, w_v, w_q, w_kv, kr_slot, kr_swapped, w_g], axis=1).astype(BF16)

    uq3 = w_uq.reshape(q_rank, HEADS, QK_HEAD)
    rope_slot, rope_swapped = _rope_slot(uq3[..., QK_NOPE:])
    nope = jnp.concatenate([uq3[..., :QK_NOPE], jnp.zeros((q_rank, HEADS, LANES - QK_NOPE), F32)], -1)
    wuq_p = jnp.concatenate([(nope + rope_slot).reshape(q_rank, HEADS * LANES),
                             rope_swapped.reshape(q_rank, HEADS * LANES)], axis=1).astype(BF16)
    uk3 = w_uk.reshape(kv_rank, HEADS, QK_NOPE)
    wuk_p = jnp.concatenate([uk3, jnp.zeros((kv_rank, HEADS, LANES - QK_NOPE), F32)], -1)
    wuk_p = wuk_p.reshape(kv_rank, HEADS * LANES).astype(BF16)
    uv3 = w_uv.reshape(kv_rank, HEADS, V_HEAD)
    wuv_p = jnp.concatenate([uv3, jnp.zeros((kv_rank, HEADS, LANES - V_HEAD), F32)], -1)
    wuv_p = wuv_p.reshape(kv_rank, HEADS * LANES).astype(BF16)
    pad = jnp.zeros((LANES - QK_HEAD,), F32)
    gq = jnp.concatenate([g_qnorm, pad]).reshape(1, LANES)
    gk = jnp.concatenate([g_knorm, pad]).reshape(1, LANES)
    gw = d_a // GMLP_GROUPS
    bias = jnp.repeat(gmlp_b[:, :chunk_len].T, gw, axis=1)
    wts = [g_mix.reshape(1, d), win_p, g_gmlp_v.reshape(1, d_a), gmlp_ws, bias,
           g_q_lat.reshape(1, q_rank), wuq_p, g_kv_lat.reshape(1, kv_rank), wuk_p, wuv_p,
           gq, gk, b_gate.reshape(1, -1), w_branch_a.astype(BF16), w_branch_b.astype(BF16),
           w_out.astype(BF16)]
    offs = _in_offsets(d_a, q_rank, kv_rank, d)
    return wts, offs


def _prompt_tile(seq):
    for tm in (512, 256, 128):
        if seq % tm == 0:
            return tm
    raise ValueError("prompt length must be a multiple of 128")


def kernel(x_prompt, x_sample, c_prompt, c_sample, cache_ckv, cache_krope, w_mod, b_mod, g_ffn1, w_ffn1_up, w_ffn1_down, g_mix, w_in, g_gmlp_v, gmlp_ws, gmlp_b, g_q_lat, w_uq, g_kv_lat, w_uk, w_uv, g_qnorm, g_knorm, b_gate, w_branch_a, w_branch_b, w_out, g_ffn2, w_ffn2_up, w_ffn2_down):
    bp, tp, d = x_prompt.shape
    bs, ts, _ = x_sample.shape
    depth = w_mod.shape[0]
    past = cache_ckv.shape[2]
    kv_rank = g_kv_lat.shape[1]
    tm = _prompt_tile(tp)
    assert ts % 8 == 0 and ts <= GMLP_CHUNK and (bs * ts) % 8 == 0

    cos_p, sin_p = _rope_tables(jnp.arange(tp, dtype=F32))
    cos_s, sin_s = _rope_tables(jnp.arange(ts, dtype=F32) + jnp.float32(past))
    c_all = jnp.concatenate([c_prompt, c_sample], axis=0)

    xp = x_prompt.reshape(bp * tp, d)
    xs = x_sample.reshape(bs * ts, d)
    outs = {k: [] for k in ("ckv_p", "kr_p", "ckv_s", "kr_s", "vg_s")}
    for l in range(depth):
        m = _modulation(c_all, w_mod[l], b_mod[l])
        m5 = m.reshape(bp + bs, 3, 3, d).transpose(1, 2, 0, 3)
        m_p = m5[:, :, :bp].reshape(3, 3, bp, 1, d)
        m_s = m5[:, :, bp:].reshape(3, 3, bs, 1, d)
        m_s_rows = jnp.repeat(m5[:, :, bp:], ts, axis=2)
        nt = tp // tm
        rows_s = bs * ts

        def ffn_pair(sub, g, w_up, w_down, xp, xs):
            w_up_b, w_down_b = w_up.astype(BF16), w_down.astype(BF16)
            p_spec = pl.BlockSpec((None, 3, None, 1, d), lambda i: (sub, 0, i // nt, 0, 0))
            s_spec = pl.BlockSpec((None, 3, rows_s, d), lambda i: (sub, 0, 0, 0))
            return (_ffn(xp, p_spec, m_p, g, w_up_b, w_down_b, tm),
                    _ffn(xs, s_spec, m_s_rows, g, w_up_b, w_down_b, rows_s))

        xp, xs = ffn_pair(0, g_ffn1[l], w_ffn1_up[l], w_ffn1_down[l], xp, xs)

        layer_args = (g_mix[l], w_in[l], g_gmlp_v[l], gmlp_ws[l], gmlp_b[l], g_q_lat[l], w_uq[l],
                      g_kv_lat[l], w_uk[l], w_uv[l], g_qnorm[l], g_knorm[l], b_gate[l],
                      w_branch_a[l], w_branch_b[l], w_out[l])
        wts_p, offs = _layer_weights(*layer_args, chunk_len=GMLP_CHUNK)
        xp, ckv_p, kr_p = _prompt_mixer(xp, m_p, cos_p, sin_p, wts_p, offs, bp, tp, tm, kv_rank)
        wts_s, _ = _layer_weights(*layer_args, chunk_len=min(ts, GMLP_CHUNK))
        past_kr_slot, _ = _rope_slot(cache_krope[l])
        xs, ckv_s, kr_s, vg_s = _sample_mixer(xs, m_s, cos_s, sin_s, cache_ckv[l], past_kr_slot,
                                              wts_s, offs, bs, ts, kv_rank)

        xp, xs = ffn_pair(2, g_ffn2[l], w_ffn2_up[l], w_ffn2_down[l], xp, xs)

        outs["ckv_p"].append(ckv_p.reshape(bp, tp, kv_rank))
        outs["kr_p"].append(kr_p.reshape(bp, tp, QK_ROPE))
        outs["ckv_s"].append(ckv_s.reshape(bs, ts, kv_rank))
        outs["kr_s"].append(kr_s.reshape(bs, ts, QK_ROPE))
        outs["vg_s"].append(vg_s.reshape(bs, ts, d))
    return (xp.reshape(bp, tp, d), xs.reshape(bs, ts, d),
            jnp.stack(outs["ckv_p"], 0), jnp.stack(outs["kr_p"], 0),
            jnp.stack(outs["ckv_s"], 0), jnp.stack(outs["kr_s"], 0), jnp.stack(outs["vg_s"], 0))
```

```python
import functools

import jax
import jax.numpy as jnp
from jax import lax
from jax.experimental import pallas as pl
from jax.experimental.pallas import tpu as pltpu

F32 = jnp.float32
BF16 = jnp.bfloat16

EPS = 1e-6
NEG = -1e30
ROPE_THETA = 10000.0
N_MOD = 9
CHUNK = 64
GMLP_CHUNK = 128
GMLP_GROUPS = 8
HEADS = 8
QK_NOPE = 64
QK_ROPE = 32
QK_HEAD = QK_NOPE + QK_ROPE
V_HEAD = 64
LANES = 128
VMEM_LIMIT = 60 * 1024 * 1024
LOG2E = 1.4426950408889634


def _in_offsets(d_a, q_rank, kv_rank, d_model):
    o_u = 0
    o_v = o_u + d_a
    o_q = o_v + d_a
    o_kv = o_q + q_rank
    o_kr = o_kv + kv_rank
    o_krs = o_kr + LANES
    o_g = o_krs + LANES
    o_end = o_g + 2 * d_model
    return o_u, o_v, o_q, o_kv, o_kr, o_krs, o_g, o_end


def _dot(a, b):
    return jnp.dot(a, b, preferred_element_type=F32)


def _dot_nt(a, b):
    return lax.dot_general(a, b, (((1,), (1,)), ((), ())), preferred_element_type=F32)


def _rms(x, g):
    ms = jnp.mean(x * x, axis=-1, keepdims=True)
    return x * lax.rsqrt(ms + EPS) * g


def _head_norm(x, g):
    ms = jnp.sum(x * x, axis=-1, keepdims=True) * (1.0 / QK_HEAD)
    return x * lax.rsqrt(ms + EPS) * g


def _mod_kernel(c_ref, w_ref, b_ref, o_ref):
    c = c_ref[...]
    a = (c * jax.nn.sigmoid(c)).astype(BF16)
    o_ref[...] = _dot(a, w_ref[...].astype(BF16)) + b_ref[...]


def _modulation(c, w_mod, b_mod):
    n, d = c.shape
    nout = w_mod.shape[1]
    bn = nout // 8
    return pl.pallas_call(
        _mod_kernel,
        grid=(nout // bn,),
        in_specs=[pl.BlockSpec((n, d), lambda j: (0, 0)),
                  pl.BlockSpec((d, bn), lambda j: (0, j)),
                  pl.BlockSpec((1, bn), lambda j: (0, j))],
        out_specs=pl.BlockSpec((n, bn), lambda j: (0, j)),
        out_shape=jax.ShapeDtypeStruct((n, nout), F32),
        compiler_params=pltpu.CompilerParams(dimension_semantics=("arbitrary",),
                                             vmem_limit_bytes=VMEM_LIMIT),
    )(c, w_mod, b_mod.reshape(1, nout))


def _ffn_kernel(x_ref, m_ref, g_ref, wup_ref, wdn_ref, o_ref, *, d_ff, chunks):
    x = x_ref[...]
    shift, scale, gate = m_ref[0], m_ref[1], m_ref[2]
    h = (_rms(x, g_ref[...]) * (1.0 + scale) + shift).astype(BF16)
    acc = None
    for c0, cw in chunks:
        a = _dot(h, wup_ref[:, c0:c0 + cw])
        b = _dot(h, wup_ref[:, d_ff + c0:d_ff + c0 + cw])
        act = (a * jax.nn.sigmoid(a) * b).astype(BF16)
        part = _dot(act, wdn_ref[c0:c0 + cw, :])
        acc = part if acc is None else acc + part
    o_ref[...] = x + (0.5 * gate) * acc


def _ffn_chunks(d_ff):
    step = 1024
    return tuple((c0, min(step, d_ff - c0)) for c0 in range(0, d_ff, step))


def _const_spec(shape):
    nd = len(shape)
    return pl.BlockSpec(shape, lambda *_: (0,) * nd, pipeline_mode=pl.Buffered(1))


def _ffn(x, m_spec, m_arr, g, w_up, w_down, tm):
    rows, d = x.shape
    d_ff = w_down.shape[0]
    kern = functools.partial(_ffn_kernel, d_ff=d_ff, chunks=_ffn_chunks(d_ff))
    return pl.pallas_call(
        kern,
        grid=(rows // tm,),
        in_specs=[pl.BlockSpec((tm, d), lambda i: (i, 0)),
                  m_spec,
                  _const_spec((1, d)),
                  _const_spec(w_up.shape),
                  _const_spec(w_down.shape)],
        out_specs=pl.BlockSpec((tm, d), lambda i: (i, 0)),
        out_shape=jax.ShapeDtypeStruct((rows, d), F32),
        compiler_params=pltpu.CompilerParams(dimension_semantics=("arbitrary",),
                                             vmem_limit_bytes=VMEM_LIMIT),
    )(x, m_arr, g.reshape(1, d), w_up, w_down)


def _tokenwise(x, m_ref, cos_ref, sin_ref, gmix_ref, win_ref, gv_ref, ws_ref, gb_ref, gql_ref,
               wuq_ref, gkv_ref, wuk_ref, wuv_ref, gqn_ref, gkn_ref, bg_ref, wa_ref, oa_scr,
               *, offs, chunk_len):
    o_u, o_v, o_q, o_kv, o_kr, o_krs, o_g, o_end = offs
    tm, d = x.shape
    shift, scale = m_ref[0], m_ref[1]
    h = (_rms(x, gmix_ref[...]) * (1.0 + scale) + shift).astype(BF16)

    u = jax.nn.gelu(_dot(h, win_ref[:, o_u:o_v]))
    v = _rms(jax.nn.gelu(_dot(h, win_ref[:, o_v:o_q])), gv_ref[...])
    vb = v.astype(BF16)
    L = chunk_len
    gw = d // GMLP_GROUPS
    row = lax.broadcasted_iota(jnp.int32, (L, L), 0)
    col = lax.broadcasted_iota(jnp.int32, (L, L), 1)
    tril = col <= row
    bias = gb_ref[...]
    for g in range(GMLP_GROUPS):
        wg = jnp.where(tril, ws_ref[g, 0:L, 0:L], 0.0).astype(BF16)
        for c in range(tm // L):
            r0 = c * L
            mixed = _dot(wg, vb[r0:r0 + L, g * gw:(g + 1) * gw]) + bias[:, g * gw:(g + 1) * gw]
            oa_scr[r0:r0 + L, g * gw:(g + 1) * gw] = (
                u[r0:r0 + L, g * gw:(g + 1) * gw] * mixed).astype(BF16)
    branch_a = _dot(oa_scr[...], wa_ref[...])

    gates = jax.nn.sigmoid(_dot(h, win_ref[:, o_g:o_end]) + bg_ref[...])
    merged_a = gates[:, :d] * branch_a
    gate_b = gates[:, d:]

    cos = cos_ref[...]
    sin = sin_ref[...]
    qn = _rms(_dot(h, win_ref[:, o_q:o_kv]), gql_ref[...]).astype(BF16)
    ckv = _rms(_dot(h, win_ref[:, o_kv:o_kr]), gkv_ref[...])
    kr2 = _dot(h, win_ref[:, o_kr:o_g])
    kr_slot = kr2[:, :LANES] * cos + kr2[:, LANES:] * sin
    ckvb = ckv.astype(BF16)
    q2 = _dot(qn, wuq_ref[...])
    kn = _dot(ckvb, wuk_ref[...])
    vv = _dot(ckvb, wuv_ref[...])
    gq = gqn_ref[...] * (QK_HEAD ** -0.5 * LOG2E)
    gk = gkn_ref[...]
    hw = HEADS * LANES
    q_heads, k_heads = [], []
    for hd in range(HEADS):
        sl = slice(hd * LANES, (hd + 1) * LANES)
        qh = q2[:, sl] * cos + q2[:, hw + hd * LANES:hw + (hd + 1) * LANES] * sin
        q_heads.append(_head_norm(qh, gq).astype(BF16))
        k_heads.append(_head_norm(kn[:, sl] + kr_slot, gk).astype(BF16))
    v_heads = _v_slots(vv)
    return merged_a, gate_b, q_heads, k_heads, v_heads, ckv, kr_slot, v


def _v_slots(vv):
    one = (lax.broadcasted_iota(jnp.int32, (1, LANES), 1) == V_HEAD).astype(F32)
    return [(vv[:, hd * LANES:(hd + 1) * LANES] + one).astype(BF16) for hd in range(HEADS)]


def _normalise(acc):
    return acc / acc[:, V_HEAD:V_HEAD + 1]


def _finish(x, gate, merged_a, gate_b, o_heads, wb_ref, wo_ref):
    tm = x.shape[0]
    lane = lax.broadcasted_iota(jnp.int32, (tm, LANES), 1)
    parts = [jnp.where(lane < V_HEAD, o_heads[2 * p],
                       pltpu.roll(o_heads[2 * p + 1], V_HEAD, axis=1)).astype(BF16)
             for p in range(HEADS // 2)]
    o_b = jnp.concatenate(parts, axis=1)
    merged = merged_a + gate_b * _dot(o_b, wb_ref[...])
    return x + gate * _dot(merged.astype(BF16), wo_ref[...])


def _prompt_mixer_kernel(x_ref, m_ref, cos_ref, sin_ref, gmix_ref, win_ref, gv_ref, ws_ref, gb_ref,
                         gql_ref, wuq_ref, gkv_ref, wuk_ref, wuv_ref, gqn_ref, gkn_ref, bg_ref,
                         wa_ref, wb_ref, wo_ref,
                         y_ref, ckv_ref, kr_ref,
                         oa_scr, k_scr, v_scr, *, offs, n_tiles):
    t = pl.program_id(1)
    x = x_ref[...]
    tm = x.shape[0]
    merged_a, gate_b, q_heads, k_heads, v_heads, ckv, kr_slot, _ = _tokenwise(
        x, m_ref, cos_ref, sin_ref, gmix_ref, win_ref, gv_ref, ws_ref, gb_ref, gql_ref, wuq_ref,
        gkv_ref, wuk_ref, wuv_ref, gqn_ref, gkn_ref, bg_ref, wa_ref, oa_scr,
        offs=offs, chunk_len=GMLP_CHUNK)
    ckv_ref[...] = ckv
    kr_ref[...] = kr_slot[:, QK_NOPE:QK_HEAD]
    row0 = pl.multiple_of(t * tm, tm)
    for hd in range(HEADS):
        k_scr[hd, pl.ds(row0, tm), :] = k_heads[hd]
        v_scr[hd, pl.ds(row0, tm), :] = v_heads[hd]

    qrow = lax.broadcasted_iota(jnp.int32, (tm, tm), 0) // CHUNK
    kcol = lax.broadcasted_iota(jnp.int32, (tm, tm), 1) // CHUNK
    diag_mask = kcol <= qrow
    ms, accs = [], []
    for hd in range(HEADS):
        s = jnp.where(diag_mask, _dot_nt(q_heads[hd], k_heads[hd]), NEG)
        m0 = jnp.max(s, axis=1, keepdims=True)
        ms.append(m0)
        accs.append(_dot(jnp.exp2(s - m0).astype(BF16), v_heads[hd]))

    for n_prev in range(n_tiles):
        @pl.when(t == n_prev)
        def _(n_prev=n_prev):
            ms_, accs_ = list(ms), list(accs)
            for j in range(n_prev):
                rows = slice(j * tm, (j + 1) * tm)
                for hd in range(HEADS):
                    sj = _dot_nt(q_heads[hd], k_scr[hd, rows, :])
                    m_new = jnp.maximum(ms_[hd], jnp.max(sj, axis=1, keepdims=True))
                    pj = jnp.exp2(sj - m_new).astype(BF16)
                    accs_[hd] = jnp.exp2(ms_[hd] - m_new) * accs_[hd] + _dot(pj, v_scr[hd, rows, :])
                    ms_[hd] = m_new
            o_heads = [_normalise(a) for a in accs_]
            y_ref[...] = _finish(x, m_ref[2], merged_a, gate_b, o_heads, wb_ref, wo_ref)


def _sample_mixer_kernel(x_ref, m_ref, cos_ref, sin_ref, pckv_ref, pkr_ref, gmix_ref, win_ref, gv_ref,
                         ws_ref, gb_ref, gql_ref, wuq_ref, gkv_ref, wuk_ref, wuv_ref, gqn_ref,
                         gkn_ref, bg_ref, wa_ref, wb_ref, wo_ref,
                         y_ref, ckv_ref, kr_ref, gv_out_ref,
                         oa_scr, k_scr, v_scr, *, offs, key_chunk):
    x = x_ref[...]
    tm = x.shape[0]
    past = pckv_ref.shape[0]
    merged_a, gate_b, q_heads, k_heads, v_heads, ckv, kr_slot, v_gmlp = _tokenwise(
        x, m_ref, cos_ref, sin_ref, gmix_ref, win_ref, gv_ref, ws_ref, gb_ref, gql_ref, wuq_ref,
        gkv_ref, wuk_ref, wuv_ref, gqn_ref, gkn_ref, bg_ref, wa_ref, oa_scr,
        offs=offs, chunk_len=min(tm, GMLP_CHUNK))
    ckv_ref[...] = ckv
    kr_ref[...] = kr_slot[:, QK_NOPE:QK_HEAD]
    gv_out_ref[...] = v_gmlp

    gk = gkn_ref[...]

    def build(i, carry):
        r = pl.multiple_of(i * key_chunk, key_chunk)
        cb = pckv_ref[pl.ds(r, key_chunk), :].astype(BF16)
        kn = _dot(cb, wuk_ref[...])
        vs = _v_slots(_dot(cb, wuv_ref[...]))
        kr = pkr_ref[pl.ds(r, key_chunk), :]
        for hd in range(HEADS):
            kh = kn[:, hd * LANES:(hd + 1) * LANES] + kr
            k_scr[hd, pl.ds(r, key_chunk), :] = _head_norm(kh, gk).astype(BF16)
            v_scr[hd, pl.ds(r, key_chunk), :] = vs[hd]
        return carry

    lax.fori_loop(0, past // key_chunk, build, 0)

    o_heads = []
    for hd in range(HEADS):
        q = q_heads[hd]
        sp = _dot_nt(q, k_scr[hd])
        sn = _dot_nt(q, k_heads[hd])
        m = jnp.maximum(jnp.max(sp, axis=1, keepdims=True), jnp.max(sn, axis=1, keepdims=True))
        pp = jnp.exp2(sp - m).astype(BF16)
        pn = jnp.exp2(sn - m).astype(BF16)
        o_heads.append(_normalise(_dot(pp, v_scr[hd]) + _dot(pn, v_heads[hd])))
    y_ref[...] = _finish(x, m_ref[2], merged_a, gate_b, o_heads, wb_ref, wo_ref)


def _mixer_weight_specs(wts):
    return [_const_spec(w.shape) for w in wts]


def _prompt_mixer(x, m_arr, cos, sin, wts, offs, batch, seq, tm, kv_rank):
    rows, d = x.shape
    nt = seq // tm
    kern = functools.partial(_prompt_mixer_kernel, offs=offs, n_tiles=nt)
    row_spec = lambda w: pl.BlockSpec((tm, w), lambda b, t: (b * nt + t, 0))
    return pl.pallas_call(
        kern,
        grid=(batch, nt),
        in_specs=[row_spec(d),
                  pl.BlockSpec((None, 3, None, 1, d), lambda b, t: (1, 0, b, 0, 0)),
                  pl.BlockSpec((tm, LANES), lambda b, t: (t, 0)),
                  pl.BlockSpec((tm, LANES), lambda b, t: (t, 0))] + _mixer_weight_specs(wts),
        out_specs=[row_spec(d), row_spec(kv_rank), row_spec(QK_ROPE)],
        out_shape=[jax.ShapeDtypeStruct((rows, d), F32),
                   jax.ShapeDtypeStruct((rows, kv_rank), F32),
                   jax.ShapeDtypeStruct((rows, QK_ROPE), F32)],
        scratch_shapes=[pltpu.VMEM((tm, d), BF16),
                        pltpu.VMEM((HEADS, seq, LANES), BF16),
                        pltpu.VMEM((HEADS, seq, LANES), BF16)],
        compiler_params=pltpu.CompilerParams(dimension_semantics=("arbitrary", "arbitrary"),
                                             vmem_limit_bytes=VMEM_LIMIT),
    )(x, m_arr, cos, sin, *wts)


def _sample_mixer(x, m_arr, cos, sin, past_ckv, past_kr_slot, wts, offs, batch, seq, kv_rank):
    rows, d = x.shape
    past = past_ckv.shape[1]
    key_chunk = 512 if past % 512 == 0 else past
    kern = functools.partial(_sample_mixer_kernel, offs=offs, key_chunk=key_chunk)
    row_spec = lambda w: pl.BlockSpec((seq, w), lambda b: (b, 0))
    return pl.pallas_call(
        kern,
        grid=(batch,),
        in_specs=[row_spec(d),
                  pl.BlockSpec((None, 3, None, 1, d), lambda b: (1, 0, b, 0, 0)),
                  pl.BlockSpec((seq, LANES), lambda b: (0, 0)),
                  pl.BlockSpec((seq, LANES), lambda b: (0, 0)),
                  pl.BlockSpec((None, past, kv_rank), lambda b: (b, 0, 0)),
                  pl.BlockSpec((None, past, LANES), lambda b: (b, 0, 0))] + _mixer_weight_specs(wts),
        out_specs=[row_spec(d), row_spec(kv_rank), row_spec(QK_ROPE), row_spec(d)],
        out_shape=[jax.ShapeDtypeStruct((rows, d), F32),
                   jax.ShapeDtypeStruct((rows, kv_rank), F32),
                   jax.ShapeDtypeStruct((rows, QK_ROPE), F32),
                   jax.ShapeDtypeStruct((rows, d), F32)],
        scratch_shapes=[pltpu.VMEM((seq, d), BF16),
                        pltpu.VMEM((HEADS, past, LANES), BF16),
                        pltpu.VMEM((HEADS, past, LANES), BF16)],
        compiler_params=pltpu.CompilerParams(dimension_semantics=("arbitrary",),
                                             vmem_limit_bytes=VMEM_LIMIT),
    )(x, m_arr, cos, sin, past_ckv, past_kr_slot, *wts)


def _rope_tables(pos):
    half = QK_ROPE // 2
    freqs = ROPE_THETA ** (-jnp.arange(half, dtype=F32) / half)
    ang = pos[:, None] * freqs[None, :]
    cos, sin = jnp.cos(ang), jnp.sin(ang)
    n = pos.shape[0]
    pad = LANES - QK_HEAD
    c = jnp.concatenate([jnp.ones((n, QK_NOPE), F32), cos, cos, jnp.zeros((n, pad), F32)], axis=1)
    s = jnp.concatenate([jnp.zeros((n, QK_NOPE), F32), -sin, sin, jnp.zeros((n, pad), F32)], axis=1)
    return c, s


def _rope_slot(w):
    half = QK_ROPE // 2
    lead = w.shape[:-1]
    z0 = jnp.zeros(lead + (QK_NOPE,), w.dtype)
    z1 = jnp.zeros(lead + (LANES - QK_HEAD,), w.dtype)
    slot = jnp.concatenate([z0, w, z1], axis=-1)
    swapped = jnp.concatenate([z0, w[..., half:], w[..., :half], z1], axis=-1)
    return slot, swapped


def _layer_weights(g_mix, w_in, g_gmlp_v, gmlp_ws, gmlp_b, g_q_lat, w_uq, g_kv_lat, w_uk, w_uv,
                   g_qnorm, g_knorm, b_gate, w_branch_a, w_branch_b, w_out, chunk_len):
    d = w_in.shape[0]
    d_a = g_gmlp_v.shape[0]
    q_rank = g_q_lat.shape[0]
    kv_rank = g_kv_lat.shape[0]
    o = 0
    w_u = w_in[:, o:o + d_a]; o += d_a
    w_v = w_in[:, o:o + d_a]; o += d_a
    w_q = w_in[:, o:o + q_rank]; o += q_rank
    w_kv = w_in[:, o:o + kv_rank]; o += kv_rank
    w_kr = w_in[:, o:o + QK_ROPE]; o += QK_ROPE
    w_g = w_in[:, o:]
    kr_slot, kr_swapped = _rope_slot(w_kr)
    win_p = jnp.concatenate([w_u, w_v, w_q, w_kv, kr_slot, kr_swapped, w_g], axis=1).astype(BF16)

    uq3 = w_uq.reshape(q_rank, HEADS, QK_HEAD)
    rope_slot, rope_swapped = _rope_slot(uq3[..., QK_NOPE:])
    nope = jnp.concatenate([uq3[..., :QK_NOPE], jnp.zeros((q_rank, HEADS, LANES - QK_NOPE), F32)], -1)
    wuq_p = jnp.concatenate([(nope + rope_slot).reshape(q_rank, HEADS * LANES),
                             rope_swapped.reshape(q_rank, HEADS * LANES)], axis=1).astype(BF16)
    uk3 = w_uk.reshape(kv_rank, HEADS, QK_NOPE)
    wuk_p = jnp.concatenate([uk3, jnp.zeros((kv_rank, HEADS, LANES - QK_NOPE), F32)], -1)
    wuk_p = wuk_p.reshape(kv_rank, HEADS * LANES).astype(BF16)
    uv3 = w_uv.reshape(kv_rank, HEADS, V_HEAD)
    wuv_p = jnp.concatenate([uv3, jnp.zeros((kv_rank, HEADS, LANES - V_HEAD), F32)], -1)
    wuv_p = wuv_p.reshape(kv_rank, HEADS * LANES).astype(BF16)
    pad = jnp.zeros((LANES - QK_HEAD,), F32)
    gq = jnp.concatenate([g_qnorm, pad]).reshape(1, LANES)
    gk = jnp.concatenate([g_knorm, pad]).reshape(1, LANES)
    gw = d_a // GMLP_GROUPS
    bias = jnp.repeat(gmlp_b[:, :chunk_len].T, gw, axis=1)
    wts = [g_mix.reshape(1, d), win_p, g_gmlp_v.reshape(1, d_a), gmlp_ws, bias,
           g_q_lat.reshape(1, q_rank), wuq_p, g_kv_lat.reshape(1, kv_rank), wuk_p, wuv_p,
           gq, gk, b_gate.reshape(1, -1), w_branch_a.astype(BF16), w_branch_b.astype(BF16),
           w_out.astype(BF16)]
    offs = _in_offsets(d_a, q_rank, kv_rank, d)
    return wts, offs


def _prompt_tile(seq):
    for tm in (512, 256, 128):
        if seq % tm == 0:
            return tm
    raise ValueError("prompt length must be a multiple of 128")


def _ffn_tile(seq):
    for tm in (1024, 512, 256, 128):
        if seq % tm == 0:
            return tm
    raise ValueError("prompt length must be a multiple of 128")


def kernel(x_prompt, x_sample, c_prompt, c_sample, cache_ckv, cache_krope, w_mod, b_mod, g_ffn1, w_ffn1_up, w_ffn1_down, g_mix, w_in, g_gmlp_v, gmlp_ws, gmlp_b, g_q_lat, w_uq, g_kv_lat, w_uk, w_uv, g_qnorm, g_knorm, b_gate, w_branch_a, w_branch_b, w_out, g_ffn2, w_ffn2_up, w_ffn2_down):
    bp, tp, d = x_prompt.shape
    bs, ts, _ = x_sample.shape
    depth = w_mod.shape[0]
    past = cache_ckv.shape[2]
    kv_rank = g_kv_lat.shape[1]
    tm = _prompt_tile(tp)
    assert ts % 8 == 0 and ts <= GMLP_CHUNK and (bs * ts) % 8 == 0

    cos_p, sin_p = _rope_tables(jnp.arange(tp, dtype=F32))
    cos_s, sin_s = _rope_tables(jnp.arange(ts, dtype=F32) + jnp.float32(past))
    c_all = jnp.concatenate([c_prompt, c_sample], axis=0)

    xp = x_prompt.reshape(bp * tp, d)
    xs = x_sample.reshape(bs * ts, d)
    outs = {k: [] for k in ("ckv_p", "kr_p", "ckv_s", "kr_s", "vg_s")}
    for l in range(depth):
        m = _modulation(c_all, w_mod[l], b_mod[l])
        m5 = m.reshape(bp + bs, 3, 3, d).transpose(1, 2, 0, 3)
        m_p = m5[:, :, :bp].reshape(3, 3, bp, 1, d)
        m_s = m5[:, :, bp:].reshape(3, 3, bs, 1, d)
        m_s_rows = jnp.repeat(m5[:, :, bp:], ts, axis=2)
        nt = tp // tm
        rows_s = bs * ts

        tm_ffn = _ffn_tile(tp)
        nt_ffn = tp // tm_ffn

        def ffn_pair(sub, g, w_up, w_down, xp, xs):
            w_up_b, w_down_b = w_up.astype(BF16), w_down.astype(BF16)
            p_spec = pl.BlockSpec((None, 3, None, 1, d), lambda i: (sub, 0, i // nt_ffn, 0, 0))
            s_spec = pl.BlockSpec((None, 3, rows_s, d), lambda i: (sub, 0, 0, 0))
            return (_ffn(xp, p_spec, m_p, g, w_up_b, w_down_b, tm_ffn),
                    _ffn(xs, s_spec, m_s_rows, g, w_up_b, w_down_b, rows_s))

        xp, xs = ffn_pair(0, g_ffn1[l], w_ffn1_up[l], w_ffn1_down[l], xp, xs)

        layer_args = (g_mix[l], w_in[l], g_gmlp_v[l], gmlp_ws[l], gmlp_b[l], g_q_lat[l], w_uq[l],
                      g_kv_lat[l], w_uk[l], w_uv[l], g_qnorm[l], g_knorm[l], b_gate[l],
                      w_branch_a[l], w_branch_b[l], w_out[l])
        wts_p, offs = _layer_weights(*layer_args, chunk_len=GMLP_CHUNK)
        xp, ckv_p, kr_p = _prompt_mixer(xp, m_p, cos_p, sin_p, wts_p, offs, bp, tp, tm, kv_rank)
        wts_s, _ = _layer_weights(*layer_args, chunk_len=min(ts, GMLP_CHUNK))
        past_kr_slot, _ = _rope_slot(cache_krope[l])
        xs, ckv_s, kr_s, vg_s = _sample_mixer(xs, m_s, cos_s, sin_s, cache_ckv[l], past_kr_slot,
                                              wts_s, offs, bs, ts, kv_rank)

        xp, xs = ffn_pair(2, g_ffn2[l], w_ffn2_up[l], w_ffn2_down[l], xp, xs)

        outs["ckv_p"].append(ckv_p.reshape(bp, tp, kv_rank))
        outs["kr_p"].append(kr_p.reshape(bp, tp, QK_ROPE))
        outs["ckv_s"].append(ckv_s.reshape(bs, ts, kv_rank))
        outs["kr_s"].append(kr_s.reshape(bs, ts, QK_ROPE))
        outs["vg_s"].append(vg_s.reshape(bs, ts, d))
    return (xp.reshape(bp, tp, d), xs.reshape(bs, ts, d),
            jnp.stack(outs["ckv_p"], 0), jnp.stack(outs["kr_p"], 0),
            jnp.stack(outs["ckv_s"], 0), jnp.stack(outs["kr_s"], 0), jnp.stack(outs["vg_s"], 0))
```

```python
import functools
from typing import Any, NamedTuple

import jax
import jax.numpy as jnp
from jax import lax
from jax.experimental import pallas as pl
from jax.experimental.pallas import tpu as pltpu

F32 = jnp.float32
BF16 = jnp.bfloat16

EPS = 1e-6
NEG = -1e30
ROPE_THETA = 10000.0
N_MOD = 9
CHUNK = 64
GMLP_CHUNK = 128
GMLP_GROUPS = 8
HEADS = 8
QK_NOPE = 64
QK_ROPE = 32
QK_HEAD = QK_NOPE + QK_ROPE
V_HEAD = 64
LANES = 128
VMEM_LIMIT = 60 * 1024 * 1024
LOG2E = 1.4426950408889634


def _in_offsets(d_a, q_rank, kv_rank):
    o_u = 0
    o_v = o_u + d_a
    o_q = o_v + d_a
    o_kv = o_q + q_rank
    o_end = o_kv + kv_rank
    return o_u, o_v, o_q, o_kv, o_end


def _dot(a, b):
    return jnp.dot(a, b, preferred_element_type=F32)


def _dot_nt(a, b):
    return lax.dot_general(a, b, (((1,), (1,)), ((), ())), preferred_element_type=F32)


def _rms(x, g):
    ms = jnp.mean(x * x, axis=-1, keepdims=True)
    return x * lax.rsqrt(ms + EPS) * g


def _head_norm(x, g):
    ms = jnp.sum(x * x, axis=-1, keepdims=True) * (1.0 / QK_HEAD)
    return x * lax.rsqrt(ms + EPS) * g


def _mod_kernel(c_ref, w_ref, b_ref, o_ref):
    c = c_ref[...]
    a = (c * jax.nn.sigmoid(c)).astype(BF16)
    o_ref[...] = _dot(a, w_ref[...].astype(BF16)) + b_ref[...]


def _modulation(c, w_mod, b_mod):
    n, d = c.shape
    nout = w_mod.shape[1]
    bn = nout // 8
    return pl.pallas_call(
        _mod_kernel,
        grid=(nout // bn,),
        in_specs=[pl.BlockSpec((n, d), lambda j: (0, 0)),
                  pl.BlockSpec((d, bn), lambda j: (0, j)),
                  pl.BlockSpec((1, bn), lambda j: (0, j))],
        out_specs=pl.BlockSpec((n, bn), lambda j: (0, j)),
        out_shape=jax.ShapeDtypeStruct((n, nout), F32),
        compiler_params=pltpu.CompilerParams(dimension_semantics=("arbitrary",),
                                             vmem_limit_bytes=VMEM_LIMIT),
    )(c, w_mod, b_mod.reshape(1, nout))


def _ffn_kernel(x_ref, m_ref, g_ref, wup_ref, wdn_ref, o_ref, *, d_ff, chunks):
    x = x_ref[...]
    shift, scale, gate = m_ref[0], m_ref[1], m_ref[2]
    h = (_rms(x, g_ref[...]) * (1.0 + scale) + shift).astype(BF16)
    acc = None
    for c0, cw in chunks:
        a = _dot(h, wup_ref[:, c0:c0 + cw])
        b = _dot(h, wup_ref[:, d_ff + c0:d_ff + c0 + cw])
        act = (a * jax.nn.sigmoid(a) * b).astype(BF16)
        part = _dot(act, wdn_ref[c0:c0 + cw, :])
        acc = part if acc is None else acc + part
    o_ref[...] = x + (0.5 * gate) * acc


def _ffn_chunks(d_ff):
    step = 1024
    return tuple((c0, min(step, d_ff - c0)) for c0 in range(0, d_ff, step))


def _const_spec(shape):
    nd = len(shape)
    return pl.BlockSpec(shape, lambda *_: (0,) * nd, pipeline_mode=pl.Buffered(1))


def _ffn(x, m_spec, m_arr, g, w_up, w_down, tm):
    rows, d = x.shape
    d_ff = w_down.shape[0]
    kern = functools.partial(_ffn_kernel, d_ff=d_ff, chunks=_ffn_chunks(d_ff))
    return pl.pallas_call(
        kern,
        grid=(rows // tm,),
        in_specs=[pl.BlockSpec((tm, d), lambda i: (i, 0)),
                  m_spec,
                  _const_spec((1, d)),
                  _const_spec(w_up.shape),
                  _const_spec(w_down.shape)],
        out_specs=pl.BlockSpec((tm, d), lambda i: (i, 0)),
        out_shape=jax.ShapeDtypeStruct((rows, d), F32),
        compiler_params=pltpu.CompilerParams(dimension_semantics=("arbitrary",),
                                             vmem_limit_bytes=VMEM_LIMIT),
    )(x, m_arr, g.reshape(1, d), w_up, w_down)


class _MixerWeights(NamedTuple):
    g_mix: Any
    w_in: Any
    w_kr: Any
    w_gate: Any
    g_gmlp_v: Any
    gmlp_ws: Any
    gmlp_bias: Any
    g_q_lat: Any
    w_uq: Any
    g_kv_lat: Any
    w_uk: Any
    w_uv: Any
    g_qnorm: Any
    g_knorm: Any
    b_gate: Any
    w_branch_a: Any
    w_branch_b: Any
    w_out: Any


N_MIXER_WEIGHTS = len(_MixerWeights._fields)


def _tokenwise(x, m_ref, cos_ref, sin_ref, w, oa_scr, *, offs, chunk_len):
    o_u, o_v, o_q, o_kv, o_end = offs
    tm, d = x.shape
    shift, scale = m_ref[0], m_ref[1]
    h = (_rms(x, w.g_mix[...]) * (1.0 + scale) + shift).astype(BF16)

    cos = cos_ref[...]
    sin = sin_ref[...]
    qn = _rms(_dot(h, w.w_in[:, o_q:o_kv]), w.g_q_lat[...]).astype(BF16)
    ckv = _rms(_dot(h, w.w_in[:, o_kv:o_end]), w.g_kv_lat[...])
    kr2 = _dot(h, w.w_kr[...])
    kr_slot = kr2[:, :LANES] * cos + kr2[:, LANES:] * sin
    ckvb = ckv.astype(BF16)
    q2 = _dot(qn, w.w_uq[...])
    kn = _dot(ckvb, w.w_uk[...])
    vv = _dot(ckvb, w.w_uv[...])
    gq = w.g_qnorm[...] * (QK_HEAD ** -0.5 * LOG2E)
    gk = w.g_knorm[...]
    hw = HEADS * LANES
    q_heads, k_heads = [], []
    for hd in range(HEADS):
        sl = slice(hd * LANES, (hd + 1) * LANES)
        qh = q2[:, sl] * cos + q2[:, hw + hd * LANES:hw + (hd + 1) * LANES] * sin
        q_heads.append(_head_norm(qh, gq).astype(BF16))
        k_heads.append(_head_norm(kn[:, sl] + kr_slot, gk).astype(BF16))
    v_heads = _v_slots(vv)

    u = jax.nn.gelu(_dot(h, w.w_in[:, o_u:o_v]))
    v = _rms(jax.nn.gelu(_dot(h, w.w_in[:, o_v:o_q])), w.g_gmlp_v[...])
    gates = jax.nn.sigmoid(_dot(h, w.w_gate[...]) + w.b_gate[...])
    vb = v.astype(BF16)
    L = chunk_len
    gw = d // GMLP_GROUPS
    row = lax.broadcasted_iota(jnp.int32, (L, L), 0)
    col = lax.broadcasted_iota(jnp.int32, (L, L), 1)
    tril = col <= row
    bias = w.gmlp_bias[...]
    for g in range(GMLP_GROUPS):
        wg = jnp.where(tril, w.gmlp_ws[g, 0:L, 0:L], 0.0).astype(BF16)
        for c in range(tm // L):
            r0 = c * L
            mixed = _dot(wg, vb[r0:r0 + L, g * gw:(g + 1) * gw]) + bias[:, g * gw:(g + 1) * gw]
            oa_scr[r0:r0 + L, g * gw:(g + 1) * gw] = (
                u[r0:r0 + L, g * gw:(g + 1) * gw] * mixed).astype(BF16)
    merged_a = gates[:, :d] * _dot(oa_scr[...], w.w_branch_a[...])
    gate_b = gates[:, d:]
    return merged_a, gate_b, q_heads, k_heads, v_heads, ckv, kr_slot, v


def _v_slots(vv):
    one = (lax.broadcasted_iota(jnp.int32, (1, LANES), 1) == V_HEAD).astype(F32)
    return [(vv[:, hd * LANES:(hd + 1) * LANES] + one).astype(BF16) for hd in range(HEADS)]


def _normalise(acc):
    return acc / acc[:, V_HEAD:V_HEAD + 1]


def _finish(x, gate, merged_a, gate_b, o_heads, w):
    wb_ref, wo_ref = w.w_branch_b, w.w_out
    tm = x.shape[0]
    lane = lax.broadcasted_iota(jnp.int32, (tm, LANES), 1)
    parts = [jnp.where(lane < V_HEAD, o_heads[2 * p],
                       pltpu.roll(o_heads[2 * p + 1], V_HEAD, axis=1)).astype(BF16)
             for p in range(HEADS // 2)]
    o_b = jnp.concatenate(parts, axis=1)
    merged = merged_a + gate_b * _dot(o_b, wb_ref[...])
    return x + gate * _dot(merged.astype(BF16), wo_ref[...])


def _prompt_mixer_kernel(x_ref, m_ref, cos_ref, sin_ref, *rest, offs, n_tiles):
    w = _MixerWeights(*rest[:N_MIXER_WEIGHTS])
    y_ref, ckv_ref, kr_ref, oa_scr, k_scr, v_scr = rest[N_MIXER_WEIGHTS:]
    t = pl.program_id(1)
    x = x_ref[...]
    tm = x.shape[0]
    merged_a, gate_b, q_heads, k_heads, v_heads, ckv, kr_slot, _ = _tokenwise(
        x, m_ref, cos_ref, sin_ref, w, oa_scr, offs=offs, chunk_len=GMLP_CHUNK)
    ckv_ref[...] = ckv
    kr_ref[...] = kr_slot[:, QK_NOPE:QK_HEAD]
    row0 = pl.multiple_of(t * tm, tm)
    for hd in range(HEADS):
        k_scr[hd, pl.ds(row0, tm), :] = k_heads[hd]
        v_scr[hd, pl.ds(row0, tm), :] = v_heads[hd]

    qrow = lax.broadcasted_iota(jnp.int32, (tm, tm), 0) // CHUNK
    kcol = lax.broadcasted_iota(jnp.int32, (tm, tm), 1) // CHUNK
    diag_mask = kcol <= qrow

    for n_prev in range(n_tiles):
        @pl.when(t == n_prev)
        def _(n_prev=n_prev):
            o_heads = []
            for hd in range(HEADS):
                sd = jnp.where(diag_mask, _dot_nt(q_heads[hd], k_heads[hd]), NEG)
                m = jnp.max(sd, axis=1, keepdims=True)
                if n_prev:
                    sp = _dot_nt(q_heads[hd], k_scr[hd, 0:n_prev * tm, :])
                    m = jnp.maximum(m, jnp.max(sp, axis=1, keepdims=True))
                    acc = _dot(jnp.exp2(sp - m).astype(BF16), v_scr[hd, 0:n_prev * tm, :])
                    acc = acc + _dot(jnp.exp2(sd - m).astype(BF16), v_heads[hd])
                else:
                    acc = _dot(jnp.exp2(sd - m).astype(BF16), v_heads[hd])
                o_heads.append(_normalise(acc))
            y_ref[...] = _finish(x, m_ref[2], merged_a, gate_b, o_heads, w)


def _sample_mixer_kernel(x_ref, m_ref, cos_ref, sin_ref, pckv_ref, pkr_ref, *rest, offs, key_chunk):
    w = _MixerWeights(*rest[:N_MIXER_WEIGHTS])
    y_ref, ckv_ref, kr_ref, gv_out_ref, oa_scr, k_scr, v_scr = rest[N_MIXER_WEIGHTS:]
    x = x_ref[...]
    tm = x.shape[0]
    past = pckv_ref.shape[0]
    merged_a, gate_b, q_heads, k_heads, v_heads, ckv, kr_slot, v_gmlp = _tokenwise(
        x, m_ref, cos_ref, sin_ref, w, oa_scr, offs=offs, chunk_len=min(tm, GMLP_CHUNK))
    ckv_ref[...] = ckv
    kr_ref[...] = kr_slot[:, QK_NOPE:QK_HEAD]
    gv_out_ref[...] = v_gmlp

    gk = w.g_knorm[...]

    def build(i, carry):
        r = pl.multiple_of(i * key_chunk, key_chunk)
        cb = pckv_ref[pl.ds(r, key_chunk), :].astype(BF16)
        kn = _dot(cb, w.w_uk[...])
        vs = _v_slots(_dot(cb, w.w_uv[...]))
        kr = pkr_ref[pl.ds(r, key_chunk), :]
        for hd in range(HEADS):
            kh = kn[:, hd * LANES:(hd + 1) * LANES] + kr
            k_scr[hd, pl.ds(r, key_chunk), :] = _head_norm(kh, gk).astype(BF16)
            v_scr[hd, pl.ds(r, key_chunk), :] = vs[hd]
        return carry

    lax.fori_loop(0, past // key_chunk, build, 0)

    o_heads = []
    for hd in range(HEADS):
        q = q_heads[hd]
        sp = _dot_nt(q, k_scr[hd])
        sn = _dot_nt(q, k_heads[hd])
        m = jnp.maximum(jnp.max(sp, axis=1, keepdims=True), jnp.max(sn, axis=1, keepdims=True))
        pp = jnp.exp2(sp - m).astype(BF16)
        pn = jnp.exp2(sn - m).astype(BF16)
        o_heads.append(_normalise(_dot(pp, v_scr[hd]) + _dot(pn, v_heads[hd])))
    y_ref[...] = _finish(x, m_ref[2], merged_a, gate_b, o_heads, w)


def _mixer_weight_specs(wts):
    return [_const_spec(w.shape) for w in wts]


def _prompt_mixer(x, m_arr, cos, sin, wts, offs, batch, seq, tm, kv_rank):
    rows, d = x.shape
    nt = seq // tm
    kern = functools.partial(_prompt_mixer_kernel, offs=offs, n_tiles=nt)
    row_spec = lambda w: pl.BlockSpec((tm, w), lambda b, t: (b * nt + t, 0))
    return pl.pallas_call(
        kern,
        grid=(batch, nt),
        in_specs=[row_spec(d),
                  pl.BlockSpec((None, 3, None, 1, d), lambda b, t: (1, 0, b, 0, 0)),
                  pl.BlockSpec((tm, LANES), lambda b, t: (t, 0)),
                  pl.BlockSpec((tm, LANES), lambda b, t: (t, 0))] + _mixer_weight_specs(wts),
        out_specs=[row_spec(d), row_spec(kv_rank), row_spec(QK_ROPE)],
        out_shape=[jax.ShapeDtypeStruct((rows, d), F32),
                   jax.ShapeDtypeStruct((rows, kv_rank), F32),
                   jax.ShapeDtypeStruct((rows, QK_ROPE), F32)],
        scratch_shapes=[pltpu.VMEM((tm, d), BF16),
                        pltpu.VMEM((HEADS, seq, LANES), BF16),
                        pltpu.VMEM((HEADS, seq, LANES), BF16)],
        compiler_params=pltpu.CompilerParams(dimension_semantics=("arbitrary", "arbitrary"),
                                             vmem_limit_bytes=VMEM_LIMIT),
    )(x, m_arr, cos, sin, *wts)


def _sample_mixer(x, m_arr, cos, sin, past_ckv, past_kr_slot, wts, offs, batch, seq, kv_rank):
    rows, d = x.shape
    past = past_ckv.shape[1]
    key_chunk = 512 if past % 512 == 0 else past
    kern = functools.partial(_sample_mixer_kernel, offs=offs, key_chunk=key_chunk)
    row_spec = lambda w: pl.BlockSpec((seq, w), lambda b: (b, 0))
    return pl.pallas_call(
        kern,
        grid=(batch,),
        in_specs=[row_spec(d),
                  pl.BlockSpec((None, 3, None, 1, d), lambda b: (1, 0, b, 0, 0)),
                  pl.BlockSpec((seq, LANES), lambda b: (0, 0)),
                  pl.BlockSpec((seq, LANES), lambda b: (0, 0)),
                  pl.BlockSpec((None, past, kv_rank), lambda b: (b, 0, 0)),
                  pl.BlockSpec((None, past, LANES), lambda b: (b, 0, 0))] + _mixer_weight_specs(wts),
        out_specs=[row_spec(d), row_spec(kv_rank), row_spec(QK_ROPE), row_spec(d)],
        out_shape=[jax.ShapeDtypeStruct((rows, d), F32),
                   jax.ShapeDtypeStruct((rows, kv_rank), F32),
                   jax.ShapeDtypeStruct((rows, QK_ROPE), F32),
                   jax.ShapeDtypeStruct((rows, d), F32)],
        scratch_shapes=[pltpu.VMEM((seq, d), BF16),
                        pltpu.VMEM((HEADS, past, LANES), BF16),
                        pltpu.VMEM((HEADS, past, LANES), BF16)],
        compiler_params=pltpu.CompilerParams(dimension_semantics=("arbitrary",),
                                             vmem_limit_bytes=VMEM_LIMIT),
    )(x, m_arr, cos, sin, past_ckv, past_kr_slot, *wts)


def _rope_tables(pos):
    half = QK_ROPE // 2
    freqs = ROPE_THETA ** (-jnp.arange(half, dtype=F32) / half)
    ang = pos[:, None] * freqs[None, :]
    cos, sin = jnp.cos(ang), jnp.sin(ang)
    n = pos.shape[0]
    pad = LANES - QK_HEAD
    c = jnp.concatenate([jnp.ones((n, QK_NOPE), F32), cos, cos, jnp.zeros((n, pad), F32)], axis=1)
    s = jnp.concatenate([jnp.zeros((n, QK_NOPE), F32), -sin, sin, jnp.zeros((n, pad), F32)], axis=1)
    return c, s


def _rope_slot(w):
    half = QK_ROPE // 2
    lead = w.shape[:-1]
    z0 = jnp.zeros(lead + (QK_NOPE,), w.dtype)
    z1 = jnp.zeros(lead + (LANES - QK_HEAD,), w.dtype)
    slot = jnp.concatenate([z0, w, z1], axis=-1)
    swapped = jnp.concatenate([z0, w[..., half:], w[..., :half], z1], axis=-1)
    return slot, swapped


def _layer_weights(g_mix, w_in, g_gmlp_v, gmlp_ws, gmlp_b, g_q_lat, w_uq, g_kv_lat, w_uk, w_uv,
                   g_qnorm, g_knorm, b_gate, w_branch_a, w_branch_b, w_out, chunk_len):
    d = w_in.shape[0]
    d_a = g_gmlp_v.shape[0]
    q_rank = g_q_lat.shape[0]
    kv_rank = g_kv_lat.shape[0]
    offs = _in_offsets(d_a, q_rank, kv_rank)
    o_kr = offs[-1]
    win_lead = w_in[:, :o_kr].astype(BF16)
    kr_slot, kr_swapped = _rope_slot(w_in[:, o_kr:o_kr + QK_ROPE])
    w_kr = jnp.concatenate([kr_slot, kr_swapped], axis=1).astype(BF16)
    w_gate = w_in[:, o_kr + QK_ROPE:].astype(BF16)

    uq3 = w_uq.reshape(q_rank, HEADS, QK_HEAD)
    rope_slot, rope_swapped = _rope_slot(uq3[..., QK_NOPE:])
    nope = jnp.concatenate([uq3[..., :QK_NOPE], jnp.zeros((q_rank, HEADS, LANES - QK_NOPE), F32)], -1)
    wuq_p = jnp.concatenate([(nope + rope_slot).reshape(q_rank, HEADS * LANES),
                             rope_swapped.reshape(q_rank, HEADS * LANES)], axis=1).astype(BF16)
    uk3 = w_uk.reshape(kv_rank, HEADS, QK_NOPE)
    wuk_p = jnp.concatenate([uk3, jnp.zeros((kv_rank, HEADS, LANES - QK_NOPE), F32)], -1)
    wuk_p = wuk_p.reshape(kv_rank, HEADS * LANES).astype(BF16)
    uv3 = w_uv.reshape(kv_rank, HEADS, V_HEAD)
    wuv_p = jnp.concatenate([uv3, jnp.zeros((kv_rank, HEADS, LANES - V_HEAD), F32)], -1)
    wuv_p = wuv_p.reshape(kv_rank, HEADS * LANES).astype(BF16)
    pad = jnp.zeros((LANES - QK_HEAD,), F32)
    gq = jnp.concatenate([g_qnorm, pad]).reshape(1, LANES)
    gk = jnp.concatenate([g_knorm, pad]).reshape(1, LANES)
    gw = d_a // GMLP_GROUPS
    bias = jnp.repeat(gmlp_b[:, :chunk_len].T, gw, axis=1)
    wts = _MixerWeights(
        g_mix=g_mix.reshape(1, d), w_in=win_lead, w_kr=w_kr, w_gate=w_gate,
        g_gmlp_v=g_gmlp_v.reshape(1, d_a), gmlp_ws=gmlp_ws, gmlp_bias=bias,
        g_q_lat=g_q_lat.reshape(1, q_rank), w_uq=wuq_p, g_kv_lat=g_kv_lat.reshape(1, kv_rank),
        w_uk=wuk_p, w_uv=wuv_p, g_qnorm=gq, g_knorm=gk, b_gate=b_gate.reshape(1, -1),
        w_branch_a=w_branch_a.astype(BF16), w_branch_b=w_branch_b.astype(BF16), w_out=w_out.astype(BF16))
    return wts, offs


def _prompt_tile(seq):
    for tm in (512, 256, 128):
        if seq % tm == 0:
            return tm
    raise ValueError("prompt length must be a multiple of 128")


def _ffn_tile(seq):
    for tm in (1024, 512, 256, 128):
        if seq % tm == 0:
            return tm
    raise ValueError("prompt length must be a multiple of 128")


def kernel(x_prompt, x_sample, c_prompt, c_sample, cache_ckv, cache_krope, w_mod, b_mod, g_ffn1, w_ffn1_up, w_ffn1_down, g_mix, w_in, g_gmlp_v, gmlp_ws, gmlp_b, g_q_lat, w_uq, g_kv_lat, w_uk, w_uv, g_qnorm, g_knorm, b_gate, w_branch_a, w_branch_b, w_out, g_ffn2, w_ffn2_up, w_ffn2_down):
    bp, tp, d = x_prompt.shape
    bs, ts, _ = x_sample.shape
    depth = w_mod.shape[0]
    past = cache_ckv.shape[2]
    kv_rank = g_kv_lat.shape[1]
    tm = _prompt_tile(tp)
    assert ts % 8 == 0 and ts <= GMLP_CHUNK and (bs * ts) % 8 == 0

    cos_p, sin_p = _rope_tables(jnp.arange(tp, dtype=F32))
    cos_s, sin_s = _rope_tables(jnp.arange(ts, dtype=F32) + jnp.float32(past))
    c_all = jnp.concatenate([c_prompt, c_sample], axis=0)

    xp = x_prompt.reshape(bp * tp, d)
    xs = x_sample.reshape(bs * ts, d)
    outs = {k: [] for k in ("ckv_p", "kr_p", "ckv_s", "kr_s", "vg_s")}
    for l in range(depth):
        m = _modulation(c_all, w_mod[l], b_mod[l])
        m5 = m.reshape(bp + bs, 3, 3, d).transpose(1, 2, 0, 3)
        m_p = m5[:, :, :bp].reshape(3, 3, bp, 1, d)
        m_s = m5[:, :, bp:].reshape(3, 3, bs, 1, d)
        m_s_rows = jnp.repeat(m5[:, :, bp:], ts, axis=2)
        nt = tp // tm
        rows_s = bs * ts

        tm_ffn = _ffn_tile(tp)
        nt_ffn = tp // tm_ffn

        def ffn_pair(sub, g, w_up, w_down, xp, xs):
            w_up_b, w_down_b = w_up.astype(BF16), w_down.astype(BF16)
            p_spec = pl.BlockSpec((None, 3, None, 1, d), lambda i: (sub, 0, i // nt_ffn, 0, 0))
            s_spec = pl.BlockSpec((None, 3, rows_s, d), lambda i: (sub, 0, 0, 0))
            return (_ffn(xp, p_spec, m_p, g, w_up_b, w_down_b, tm_ffn),
                    _ffn(xs, s_spec, m_s_rows, g, w_up_b, w_down_b, rows_s))

        xp, xs = ffn_pair(0, g_ffn1[l], w_ffn1_up[l], w_ffn1_down[l], xp, xs)

        layer_args = (g_mix[l], w_in[l], g_gmlp_v[l], gmlp_ws[l], gmlp_b[l], g_q_lat[l], w_uq[l],
                      g_kv_lat[l], w_uk[l], w_uv[l], g_qnorm[l], g_knorm[l], b_gate[l],
                      w_branch_a[l], w_branch_b[l], w_out[l])
        wts_p, offs = _layer_weights(*layer_args, chunk_len=GMLP_CHUNK)
        xp, ckv_p, kr_p = _prompt_mixer(xp, m_p, cos_p, sin_p, wts_p, offs, bp, tp, tm, kv_rank)
        wts_s, _ = _layer_weights(*layer_args, chunk_len=min(ts, GMLP_CHUNK))
        past_kr_slot, _ = _rope_slot(cache_krope[l])
        xs, ckv_s, kr_s, vg_s = _sample_mixer(xs, m_s, cos_s, sin_s, cache_ckv[l], past_kr_slot,
                                              wts_s, offs, bs, ts, kv_rank)

        xp, xs = ffn_pair(2, g_ffn2[l], w_ffn2_up[l], w_ffn2_down[l], xp, xs)

        outs["ckv_p"].append(ckv_p.reshape(bp, tp, kv_rank))
        outs["kr_p"].append(kr_p.reshape(bp, tp, QK_ROPE))
        outs["ckv_s"].append(ckv_s.reshape(bs, ts, kv_rank))
        outs["kr_s"].append(kr_s.reshape(bs, ts, QK_ROPE))
        outs["vg_s"].append(vg_s.reshape(bs, ts, d))
    return (xp.reshape(bp, tp, d), xs.reshape(bs, ts, d),
            jnp.stack(outs["ckv_p"], 0), jnp.stack(outs["kr_p"], 0),
            jnp.stack(outs["ckv_s"], 0), jnp.stack(outs["kr_s"], 0), jnp.stack(outs["vg_s"], 0))
```

```python
import functools
from typing import Any, NamedTuple

import jax
import jax.numpy as jnp
from jax import lax
from jax.experimental import pallas as pl
from jax.experimental.pallas import tpu as pltpu

F32 = jnp.float32
BF16 = jnp.bfloat16

EPS = 1e-6
NEG = -1e30
ROPE_THETA = 10000.0
N_MOD = 9
CHUNK = 64
GMLP_CHUNK = 128
GMLP_GROUPS = 8
HEADS = 8
QK_NOPE = 64
QK_ROPE = 32
QK_HEAD = QK_NOPE + QK_ROPE
V_HEAD = 64
LANES = 128
VMEM_LIMIT = 60 * 1024 * 1024
LOG2E = 1.4426950408889634


def _in_offsets(d_a, q_rank, kv_rank):
    o_u = 0
    o_v = o_u + d_a
    o_q = o_v + d_a
    o_kv = o_q + q_rank
    o_end = o_kv + kv_rank
    return o_u, o_v, o_q, o_kv, o_end


def _dot(a, b):
    return jnp.dot(a, b, preferred_element_type=F32)


def _dot_nt(a, b):
    return lax.dot_general(a, b, (((1,), (1,)), ((), ())), preferred_element_type=F32)


def _rms(x, g):
    ms = jnp.mean(x * x, axis=-1, keepdims=True)
    return x * lax.rsqrt(ms + EPS) * g


def _head_norm(x, g):
    ms = jnp.sum(x * x, axis=-1, keepdims=True) * (1.0 / QK_HEAD)
    return x * lax.rsqrt(ms + EPS) * g


def _mod_kernel(c_ref, w_ref, b_ref, o_ref):
    c = c_ref[...]
    a = (c * jax.nn.sigmoid(c)).astype(BF16)
    o_ref[...] = _dot(a, w_ref[...].astype(BF16)) + b_ref[...]


def _modulation(c, w_mod, b_mod):
    n, d = c.shape
    nout = w_mod.shape[1]
    bn = nout // 8
    return pl.pallas_call(
        _mod_kernel,
        grid=(nout // bn,),
        in_specs=[pl.BlockSpec((n, d), lambda j: (0, 0)),
                  pl.BlockSpec((d, bn), lambda j: (0, j)),
                  pl.BlockSpec((1, bn), lambda j: (0, j))],
        out_specs=pl.BlockSpec((n, bn), lambda j: (0, j)),
        out_shape=jax.ShapeDtypeStruct((n, nout), F32),
        compiler_params=pltpu.CompilerParams(dimension_semantics=("arbitrary",),
                                             vmem_limit_bytes=VMEM_LIMIT),
    )(c, w_mod, b_mod.reshape(1, nout))


def _ffn_kernel(x_ref, m_ref, g_ref, wup_ref, wdn_ref, o_ref, *, d_ff, chunks):
    x = x_ref[...]
    shift, scale, gate = m_ref[0], m_ref[1], m_ref[2]
    h = (_rms(x, g_ref[...]) * (1.0 + scale) + shift).astype(BF16)
    acc = None
    for c0, cw in chunks:
        a = _dot(h, wup_ref[:, c0:c0 + cw])
        b = _dot(h, wup_ref[:, d_ff + c0:d_ff + c0 + cw])
        act = (a * jax.nn.sigmoid(a) * b).astype(BF16)
        part = _dot(act, wdn_ref[c0:c0 + cw, :])
        acc = part if acc is None else acc + part
    o_ref[...] = x + (0.5 * gate) * acc


def _ffn_chunks(d_ff):
    step = 1024
    return tuple((c0, min(step, d_ff - c0)) for c0 in range(0, d_ff, step))


def _const_spec(shape):
    nd = len(shape)
    return pl.BlockSpec(shape, lambda *_: (0,) * nd, pipeline_mode=pl.Buffered(1))


def _ffn(x, m_spec, m_arr, g, w_up, w_down, tm):
    rows, d = x.shape
    d_ff = w_down.shape[0]
    kern = functools.partial(_ffn_kernel, d_ff=d_ff, chunks=_ffn_chunks(d_ff))
    return pl.pallas_call(
        kern,
        grid=(rows // tm,),
        in_specs=[pl.BlockSpec((tm, d), lambda i: (i, 0)),
                  m_spec,
                  _const_spec((1, d)),
                  _const_spec(w_up.shape),
                  _const_spec(w_down.shape)],
        out_specs=pl.BlockSpec((tm, d), lambda i: (i, 0)),
        out_shape=jax.ShapeDtypeStruct((rows, d), F32),
        compiler_params=pltpu.CompilerParams(dimension_semantics=("arbitrary",),
                                             vmem_limit_bytes=VMEM_LIMIT),
    )(x, m_arr, g.reshape(1, d), w_up, w_down)


class _MixerWeights(NamedTuple):
    g_mix: Any
    w_in: Any
    w_kr: Any
    w_gate: Any
    g_gmlp_v: Any
    gmlp_ws: Any
    gmlp_bias: Any
    g_q_lat: Any
    w_uq: Any
    g_kv_lat: Any
    w_uk: Any
    w_uv: Any
    g_qnorm: Any
    g_knorm: Any
    b_gate: Any
    w_branch_a: Any
    w_branch_b: Any
    w_out: Any


N_MIXER_WEIGHTS = len(_MixerWeights._fields)


def _tokenwise(x, m_ref, cos_ref, sin_ref, w, oa_scr, *, offs, chunk_len):
    o_u, o_v, o_q, o_kv, o_end = offs
    tm, d = x.shape
    shift, scale = m_ref[0], m_ref[1]
    h = (_rms(x, w.g_mix[...]) * (1.0 + scale) + shift).astype(BF16)

    cos = cos_ref[...]
    sin = sin_ref[...]
    qn = _rms(_dot(h, w.w_in[:, o_q:o_kv]), w.g_q_lat[...]).astype(BF16)
    ckv = _rms(_dot(h, w.w_in[:, o_kv:o_end]), w.g_kv_lat[...])
    kr2 = _dot(h, w.w_kr[...])
    kr_slot = kr2[:, :LANES] * cos + kr2[:, LANES:] * sin
    ckvb = ckv.astype(BF16)
    q2 = _dot(qn, w.w_uq[...])
    kn = _dot(ckvb, w.w_uk[...])
    vv = _dot(ckvb, w.w_uv[...])
    gq = w.g_qnorm[...] * (QK_HEAD ** -0.5 * LOG2E)
    gk = w.g_knorm[...]
    hw = HEADS * LANES
    q_heads, k_heads = [], []
    for hd in range(HEADS):
        sl = slice(hd * LANES, (hd + 1) * LANES)
        qh = q2[:, sl] * cos + q2[:, hw + hd * LANES:hw + (hd + 1) * LANES] * sin
        q_heads.append(_head_norm(qh, gq).astype(BF16))
        k_heads.append(_head_norm(kn[:, sl] + kr_slot, gk).astype(BF16))
    v_heads = _v_slots(vv)

    u = jax.nn.gelu(_dot(h, w.w_in[:, o_u:o_v]))
    v = _rms(jax.nn.gelu(_dot(h, w.w_in[:, o_v:o_q])), w.g_gmlp_v[...])
    gates = jax.nn.sigmoid(_dot(h, w.w_gate[...]) + w.b_gate[...])
    vb = v.astype(BF16)
    L = chunk_len
    gw = d // GMLP_GROUPS
    row = lax.broadcasted_iota(jnp.int32, (L, L), 0)
    col = lax.broadcasted_iota(jnp.int32, (L, L), 1)
    tril = col <= row
    bias = w.gmlp_bias[...]
    for g in range(GMLP_GROUPS):
        wg = jnp.where(tril, w.gmlp_ws[g, 0:L, 0:L], 0.0).astype(BF16)
        for c in range(tm // L):
            r0 = c * L
            mixed = _dot(wg, vb[r0:r0 + L, g * gw:(g + 1) * gw]) + bias[:, g * gw:(g + 1) * gw]
            oa_scr[r0:r0 + L, g * gw:(g + 1) * gw] = (
                u[r0:r0 + L, g * gw:(g + 1) * gw] * mixed).astype(BF16)
    merged_a = gates[:, :d] * _dot(oa_scr[...], w.w_branch_a[...])
    gate_b = gates[:, d:]
    return merged_a, gate_b, q_heads, k_heads, v_heads, ckv, kr_slot, v


def _v_slots(vv):
    one = (lax.broadcasted_iota(jnp.int32, (1, LANES), 1) == V_HEAD).astype(F32)
    return [(vv[:, hd * LANES:(hd + 1) * LANES] + one).astype(BF16) for hd in range(HEADS)]


def _normalise(acc):
    return acc / acc[:, V_HEAD:V_HEAD + 1]


def _finish(x, gate, merged_a, gate_b, o_heads, w):
    wb_ref, wo_ref = w.w_branch_b, w.w_out
    tm = x.shape[0]
    lane = lax.broadcasted_iota(jnp.int32, (tm, LANES), 1)
    parts = [jnp.where(lane < V_HEAD, o_heads[2 * p],
                       pltpu.roll(o_heads[2 * p + 1], V_HEAD, axis=1)).astype(BF16)
             for p in range(HEADS // 2)]
    o_b = jnp.concatenate(parts, axis=1)
    merged = merged_a + gate_b * _dot(o_b, wb_ref[...])
    return x + gate * _dot(merged.astype(BF16), wo_ref[...])


def _prompt_mixer_kernel(x_ref, m_ref, cos_ref, sin_ref, *rest, offs, n_tiles):
    w = _MixerWeights(*rest[:N_MIXER_WEIGHTS])
    y_ref, ckv_ref, kr_ref, oa_scr, k_scr, v_scr = rest[N_MIXER_WEIGHTS:]
    t = pl.program_id(1)
    x = x_ref[...]
    tm = x.shape[0]
    merged_a, gate_b, q_heads, k_heads, v_heads, ckv, kr_slot, _ = _tokenwise(
        x, m_ref, cos_ref, sin_ref, w, oa_scr, offs=offs, chunk_len=GMLP_CHUNK)
    ckv_ref[...] = ckv
    kr_ref[...] = kr_slot[:, QK_NOPE:QK_HEAD]
    row0 = pl.multiple_of(t * tm, tm)
    for hd in range(HEADS):
        k_scr[hd, pl.ds(row0, tm), :] = k_heads[hd]
        v_scr[hd, pl.ds(row0, tm), :] = v_heads[hd]

    qrow = lax.broadcasted_iota(jnp.int32, (tm, tm), 0) // CHUNK
    kcol = lax.broadcasted_iota(jnp.int32, (tm, tm), 1) // CHUNK
    diag_mask = kcol <= qrow
    ms, accs = [], []
    for hd in range(HEADS):
        s = jnp.where(diag_mask, _dot_nt(q_heads[hd], k_heads[hd]), NEG)
        m0 = jnp.max(s, axis=1, keepdims=True)
        ms.append(m0)
        accs.append(_dot(jnp.exp2(s - m0).astype(BF16), v_heads[hd]))

    for n_prev in range(n_tiles):
        @pl.when(t == n_prev)
        def _(n_prev=n_prev):
            ms_, accs_ = list(ms), list(accs)
            for j in range(n_prev):
                rows = slice(j * tm, (j + 1) * tm)
                for hd in range(HEADS):
                    sj = _dot_nt(q_heads[hd], k_scr[hd, rows, :])
                    m_new = jnp.maximum(ms_[hd], jnp.max(sj, axis=1, keepdims=True))
                    pj = jnp.exp2(sj - m_new).astype(BF16)
                    accs_[hd] = jnp.exp2(ms_[hd] - m_new) * accs_[hd] + _dot(pj, v_scr[hd, rows, :])
                    ms_[hd] = m_new
            o_heads = [_normalise(a) for a in accs_]
            y_ref[...] = _finish(x, m_ref[2], merged_a, gate_b, o_heads, w)


def _sample_mixer_kernel(x_ref, m_ref, cos_ref, sin_ref, pckv_ref, pkr_ref, *rest, offs, key_chunk):
    w = _MixerWeights(*rest[:N_MIXER_WEIGHTS])
    y_ref, ckv_ref, kr_ref, gv_out_ref, oa_scr, k_scr, v_scr = rest[N_MIXER_WEIGHTS:]
    x = x_ref[...]
    tm = x.shape[0]
    past = pckv_ref.shape[0]
    merged_a, gate_b, q_heads, k_heads, v_heads, ckv, kr_slot, v_gmlp = _tokenwise(
        x, m_ref, cos_ref, sin_ref, w, oa_scr, offs=offs, chunk_len=min(tm, GMLP_CHUNK))
    ckv_ref[...] = ckv
    kr_ref[...] = kr_slot[:, QK_NOPE:QK_HEAD]
    gv_out_ref[...] = v_gmlp

    gk = w.g_knorm[...]

    def build(i, carry):
        r = pl.multiple_of(i * key_chunk, key_chunk)
        cb = pckv_ref[pl.ds(r, key_chunk), :].astype(BF16)
        kn = _dot(cb, w.w_uk[...])
        vs = _v_slots(_dot(cb, w.w_uv[...]))
        kr = pkr_ref[pl.ds(r, key_chunk), :]
        for hd in range(HEADS):
            kh = kn[:, hd * LANES:(hd + 1) * LANES] + kr
            k_scr[hd, pl.ds(r, key_chunk), :] = _head_norm(kh, gk).astype(BF16)
            v_scr[hd, pl.ds(r, key_chunk), :] = vs[hd]
        return carry

    lax.fori_loop(0, past // key_chunk, build, 0)

    o_heads = []
    for hd in range(HEADS):
        q = q_heads[hd]
        sp = _dot_nt(q, k_scr[hd])
        sn = _dot_nt(q, k_heads[hd])
        m = jnp.maximum(jnp.max(sp, axis=1, keepdims=True), jnp.max(sn, axis=1, keepdims=True))
        pp = jnp.exp2(sp - m).astype(BF16)
        pn = jnp.exp2(sn - m).astype(BF16)
        o_heads.append(_normalise(_dot(pp, v_scr[hd]) + _dot(pn, v_heads[hd])))
    y_ref[...] = _finish(x, m_ref[2], merged_a, gate_b, o_heads, w)


def _mixer_weight_specs(wts):
    return [_const_spec(w.shape) for w in wts]


def _prompt_mixer(x, m_arr, cos, sin, wts, offs, batch, seq, tm, kv_rank):
    rows, d = x.shape
    nt = seq // tm
    kern = functools.partial(_prompt_mixer_kernel, offs=offs, n_tiles=nt)
    row_spec = lambda w: pl.BlockSpec((tm, w), lambda b, t: (b * nt + t, 0))
    return pl.pallas_call(
        kern,
        grid=(batch, nt),
        in_specs=[row_spec(d),
                  pl.BlockSpec((None, 3, None, 1, d), lambda b, t: (1, 0, b, 0, 0)),
                  pl.BlockSpec((tm, LANES), lambda b, t: (t, 0)),
                  pl.BlockSpec((tm, LANES), lambda b, t: (t, 0))] + _mixer_weight_specs(wts),
        out_specs=[row_spec(d), row_spec(kv_rank), row_spec(QK_ROPE)],
        out_shape=[jax.ShapeDtypeStruct((rows, d), F32),
                   jax.ShapeDtypeStruct((rows, kv_rank), F32),
                   jax.ShapeDtypeStruct((rows, QK_ROPE), F32)],
        scratch_shapes=[pltpu.VMEM((tm, d), BF16),
                        pltpu.VMEM((HEADS, seq, LANES), BF16),
                        pltpu.VMEM((HEADS, seq, LANES), BF16)],
        compiler_params=pltpu.CompilerParams(dimension_semantics=("arbitrary", "arbitrary"),
                                             vmem_limit_bytes=VMEM_LIMIT),
    )(x, m_arr, cos, sin, *wts)


def _sample_mixer(x, m_arr, cos, sin, past_ckv, past_kr_slot, wts, offs, batch, seq, kv_rank):
    rows, d = x.shape
    past = past_ckv.shape[1]
    key_chunk = 512 if past % 512 == 0 else past
    kern = functools.partial(_sample_mixer_kernel, offs=offs, key_chunk=key_chunk)
    row_spec = lambda w: pl.BlockSpec((seq, w), lambda b: (b, 0))
    return pl.pallas_call(
        kern,
        grid=(batch,),
        in_specs=[row_spec(d),
                  pl.BlockSpec((None, 3, None, 1, d), lambda b: (1, 0, b, 0, 0)),
                  pl.BlockSpec((seq, LANES), lambda b: (0, 0)),
                  pl.BlockSpec((seq, LANES), lambda b: (0, 0)),
                  pl.BlockSpec((None, past, kv_rank), lambda b: (b, 0, 0)),
                  pl.BlockSpec((None, past, LANES), lambda b: (b, 0, 0))] + _mixer_weight_specs(wts),
        out_specs=[row_spec(d), row_spec(kv_rank), row_spec(QK_ROPE), row_spec(d)],
        out_shape=[jax.ShapeDtypeStruct((rows, d), F32),
                   jax.ShapeDtypeStruct((rows, kv_rank), F32),
                   jax.ShapeDtypeStruct((rows, QK_ROPE), F32),
                   jax.ShapeDtypeStruct((rows, d), F32)],
        scratch_shapes=[pltpu.VMEM((seq, d), BF16),
                        pltpu.VMEM((HEADS, past, LANES), BF16),
                        pltpu.VMEM((HEADS, past, LANES), BF16)],
        compiler_params=pltpu.CompilerParams(dimension_semantics=("arbitrary",),
                                             vmem_limit_bytes=VMEM_LIMIT),
    )(x, m_arr, cos, sin, past_ckv, past_kr_slot, *wts)


def _rope_tables(pos):
    half = QK_ROPE // 2
    freqs = ROPE_THETA ** (-jnp.arange(half, dtype=F32) / half)
    ang = pos[:, None] * freqs[None, :]
    cos, sin = jnp.cos(ang), jnp.sin(ang)
    n = pos.shape[0]
    pad = LANES - QK_HEAD
    c = jnp.concatenate([jnp.ones((n, QK_NOPE), F32), cos, cos, jnp.zeros((n, pad), F32)], axis=1)
    s = jnp.concatenate([jnp.zeros((n, QK_NOPE), F32), -sin, sin, jnp.zeros((n, pad), F32)], axis=1)
    return c, s


def _rope_slot(w):
    half = QK_ROPE // 2
    lead = w.shape[:-1]
    z0 = jnp.zeros(lead + (QK_NOPE,), w.dtype)
    z1 = jnp.zeros(lead + (LANES - QK_HEAD,), w.dtype)
    slot = jnp.concatenate([z0, w, z1], axis=-1)
    swapped = jnp.concatenate([z0, w[..., half:], w[..., :half], z1], axis=-1)
    return slot, swapped


def _layer_weights(g_mix, w_in, g_gmlp_v, gmlp_ws, gmlp_b, g_q_lat, w_uq, g_kv_lat, w_uk, w_uv,
                   g_qnorm, g_knorm, b_gate, w_branch_a, w_branch_b, w_out, chunk_len):
    d = w_in.shape[0]
    d_a = g_gmlp_v.shape[0]
    q_rank = g_q_lat.shape[0]
    kv_rank = g_kv_lat.shape[0]
    offs = _in_offsets(d_a, q_rank, kv_rank)
    o_kr = offs[-1]
    win_lead = w_in[:, :o_kr].astype(BF16)
    kr_slot, kr_swapped = _rope_slot(w_in[:, o_kr:o_kr + QK_ROPE])
    w_kr = jnp.concatenate([kr_slot, kr_swapped], axis=1).astype(BF16)
    w_gate = w_in[:, o_kr + QK_ROPE:].astype(BF16)

    uq3 = w_uq.reshape(q_rank, HEADS, QK_HEAD)
    rope_slot, rope_swapped = _rope_slot(uq3[..., QK_NOPE:])
    nope = jnp.concatenate([uq3[..., :QK_NOPE], jnp.zeros((q_rank, HEADS, LANES - QK_NOPE), F32)], -1)
    wuq_p = jnp.concatenate([(nope + rope_slot).reshape(q_rank, HEADS * LANES),
                             rope_swapped.reshape(q_rank, HEADS * LANES)], axis=1).astype(BF16)
    uk3 = w_uk.reshape(kv_rank, HEADS, QK_NOPE)
    wuk_p = jnp.concatenate([uk3, jnp.zeros((kv_rank, HEADS, LANES - QK_NOPE), F32)], -1)
    wuk_p = wuk_p.reshape(kv_rank, HEADS * LANES).astype(BF16)
    uv3 = w_uv.reshape(kv_rank, HEADS, V_HEAD)
    wuv_p = jnp.concatenate([uv3, jnp.zeros((kv_rank, HEADS, LANES - V_HEAD), F32)], -1)
    wuv_p = wuv_p.reshape(kv_rank, HEADS * LANES).astype(BF16)
    pad = jnp.zeros((LANES - QK_HEAD,), F32)
    gq = jnp.concatenate([g_qnorm, pad]).reshape(1, LANES)
    gk = jnp.concatenate([g_knorm, pad]).reshape(1, LANES)
    gw = d_a // GMLP_GROUPS
    bias = jnp.repeat(gmlp_b[:, :chunk_len].T, gw, axis=1)
    wts = _MixerWeights(
        g_mix=g_mix.reshape(1, d), w_in=win_lead, w_kr=w_kr, w_gate=w_gate,
        g_gmlp_v=g_gmlp_v.reshape(1, d_a), gmlp_ws=gmlp_ws, gmlp_bias=bias,
        g_q_lat=g_q_lat.reshape(1, q_rank), w_uq=wuq_p, g_kv_lat=g_kv_lat.reshape(1, kv_rank),
        w_uk=wuk_p, w_uv=wuv_p, g_qnorm=gq, g_knorm=gk, b_gate=b_gate.reshape(1, -1),
        w_branch_a=w_branch_a.astype(BF16), w_branch_b=w_branch_b.astype(BF16), w_out=w_out.astype(BF16))
    return wts, offs


def _prompt_tile(seq):
    for tm in (512, 256, 128):
        if seq % tm == 0:
            return tm
    raise ValueError("prompt length must be a multiple of 128")


def _ffn_tile(seq):
    for tm in (1024, 512, 256, 128):
        if seq % tm == 0:
            return tm
    raise ValueError("prompt length must be a multiple of 128")


def kernel(x_prompt, x_sample, c_prompt, c_sample, cache_ckv, cache_krope, w_mod, b_mod, g_ffn1, w_ffn1_up, w_ffn1_down, g_mix, w_in, g_gmlp_v, gmlp_ws, gmlp_b, g_q_lat, w_uq, g_kv_lat, w_uk, w_uv, g_qnorm, g_knorm, b_gate, w_branch_a, w_branch_b, w_out, g_ffn2, w_ffn2_up, w_ffn2_down):
    bp, tp, d = x_prompt.shape
    bs, ts, _ = x_sample.shape
    depth = w_mod.shape[0]
    past = cache_ckv.shape[2]
    kv_rank = g_kv_lat.shape[1]
    tm = _prompt_tile(tp)
    assert ts % 8 == 0 and ts <= GMLP_CHUNK and (bs * ts) % 8 == 0

    cos_p, sin_p = _rope_tables(jnp.arange(tp, dtype=F32))
    cos_s, sin_s = _rope_tables(jnp.arange(ts, dtype=F32) + jnp.float32(past))
    c_all = jnp.concatenate([c_prompt, c_sample], axis=0)

    xp = x_prompt.reshape(bp * tp, d)
    xs = x_sample.reshape(bs * ts, d)
    outs = {k: [] for k in ("ckv_p", "kr_p", "ckv_s", "kr_s", "vg_s")}
    for l in range(depth):
        m = _modulation(c_all, w_mod[l], b_mod[l])
        m5 = m.reshape(bp + bs, 3, 3, d).transpose(1, 2, 0, 3)
        m_p = m5[:, :, :bp].reshape(3, 3, bp, 1, d)
        m_s = m5[:, :, bp:].reshape(3, 3, bs, 1, d)
        m_s_rows = jnp.repeat(m5[:, :, bp:], ts, axis=2)
        nt = tp // tm
        rows_s = bs * ts

        tm_ffn = _ffn_tile(tp)
        nt_ffn = tp // tm_ffn

        def ffn_pair(sub, g, w_up, w_down, xp, xs):
            w_up_b, w_down_b = w_up.astype(BF16), w_down.astype(BF16)
            p_spec = pl.BlockSpec((None, 3, None, 1, d), lambda i: (sub, 0, i // nt_ffn, 0, 0))
            s_spec = pl.BlockSpec((None, 3, rows_s, d), lambda i: (sub, 0, 0, 0))
            return (_ffn(xp, p_spec, m_p, g, w_up_b, w_down_b, tm_ffn),
                    _ffn(xs, s_spec, m_s_rows, g, w_up_b, w_down_b, rows_s))

        xp, xs = ffn_pair(0, g_ffn1[l], w_ffn1_up[l], w_ffn1_down[l], xp, xs)

        layer_args = (g_mix[l], w_in[l], g_gmlp_v[l], gmlp_ws[l], gmlp_b[l], g_q_lat[l], w_uq[l],
                      g_kv_lat[l], w_uk[l], w_uv[l], g_qnorm[l], g_knorm[l], b_gate[l],
                      w_branch_a[l], w_branch_b[l], w_out[l])
        wts_p, offs = _layer_weights(*layer_args, chunk_len=GMLP_CHUNK)
        xp, ckv_p, kr_p = _prompt_mixer(xp, m_p, cos_p, sin_p, wts_p, offs, bp, tp, tm, kv_rank)
        wts_s, _ = _layer_weights(*layer_args, chunk_len=min(ts, GMLP_CHUNK))
        past_kr_slot, _ = _rope_slot(cache_krope[l])
        xs, ckv_s, kr_s, vg_s = _sample_mixer(xs, m_s, cos_s, sin_s, cache_ckv[l], past_kr_slot,
                                              wts_s, offs, bs, ts, kv_rank)

        xp, xs = ffn_pair(2, g_ffn2[l], w_ffn2_up[l], w_ffn2_down[l], xp, xs)

        outs["ckv_p"].append(ckv_p.reshape(bp, tp, kv_rank))
        outs["kr_p"].append(kr_p.reshape(bp, tp, QK_ROPE))
        outs["ckv_s"].append(ckv_s.reshape(bs, ts, kv_rank))
        outs["kr_s"].append(kr_s.reshape(bs, ts, QK_ROPE))
        outs["vg_s"].append(vg_s.reshape(bs, ts, d))
    return (xp.reshape(bp, tp, d), xs.reshape(bs, ts, d),
            jnp.stack(outs["ckv_p"], 0), jnp.stack(outs["kr_p"], 0),
            jnp.stack(outs["ckv_s"], 0), jnp.stack(outs["kr_s"], 0), jnp.stack(outs["vg_s"], 0))
```

```python
import functools
from typing import Any, NamedTuple

import jax
import jax.numpy as jnp
from jax import lax
from jax.experimental import pallas as pl
from jax.experimental.pallas import tpu as pltpu

F32 = jnp.float32
BF16 = jnp.bfloat16

EPS = 1e-6
NEG = -1e30
ROPE_THETA = 10000.0
N_MOD = 9
CHUNK = 64
GMLP_CHUNK = 128
GMLP_GROUPS = 8
HEADS = 8
QK_NOPE = 64
QK_ROPE = 32
QK_HEAD = QK_NOPE + QK_ROPE
V_HEAD = 64
LANES = 128
VMEM_LIMIT = 60 * 1024 * 1024
LOG2E = 1.4426950408889634


def _in_offsets(d_a, q_rank, kv_rank):
    o_u = 0
    o_v = o_u + d_a
    o_q = o_v + d_a
    o_kv = o_q + q_rank
    o_end = o_kv + kv_rank
    return o_u, o_v, o_q, o_kv, o_end


def _dot(a, b):
    return jnp.dot(a, b, preferred_element_type=F32)


def _dot_nt(a, b):
    return lax.dot_general(a, b, (((1,), (1,)), ((), ())), preferred_element_type=F32)


def _rms(x, g):
    ms = jnp.mean(x * x, axis=-1, keepdims=True)
    return x * lax.rsqrt(ms + EPS) * g


def _head_norm(x, g):
    ms = jnp.sum(x * x, axis=-1, keepdims=True) * (1.0 / QK_HEAD)
    return x * lax.rsqrt(ms + EPS) * g


def _mod_kernel(c_ref, w_ref, b_ref, o_ref):
    c = c_ref[...]
    a = (c * jax.nn.sigmoid(c)).astype(BF16)
    o_ref[...] = _dot(a, w_ref[...].astype(BF16)) + b_ref[...]


def _modulation(c, w_mod, b_mod):
    n, d = c.shape
    nout = w_mod.shape[1]
    bn = nout // 8
    return pl.pallas_call(
        _mod_kernel,
        grid=(nout // bn,),
        in_specs=[pl.BlockSpec((n, d), lambda j: (0, 0)),
                  pl.BlockSpec((d, bn), lambda j: (0, j)),
                  pl.BlockSpec((1, bn), lambda j: (0, j))],
        out_specs=pl.BlockSpec((n, bn), lambda j: (0, j)),
        out_shape=jax.ShapeDtypeStruct((n, nout), F32),
        compiler_params=pltpu.CompilerParams(dimension_semantics=("arbitrary",),
                                             vmem_limit_bytes=VMEM_LIMIT),
    )(c, w_mod, b_mod.reshape(1, nout))


def _ffn_kernel(x_ref, m_ref, g_ref, wup_ref, wdn_ref, o_ref, *, d_ff, chunks):
    x = x_ref[...]
    shift, scale, gate = m_ref[0], m_ref[1], m_ref[2]
    h = (_rms(x, g_ref[...]) * (1.0 + scale) + shift).astype(BF16)
    acc = None
    for c0, cw in chunks:
        a = _dot(h, wup_ref[:, c0:c0 + cw])
        b = _dot(h, wup_ref[:, d_ff + c0:d_ff + c0 + cw])
        act = (a * jax.nn.sigmoid(a) * b).astype(BF16)
        part = _dot(act, wdn_ref[c0:c0 + cw, :])
        acc = part if acc is None else acc + part
    o_ref[...] = x + (0.5 * gate) * acc


def _ffn_chunks(d_ff):
    step = 1024
    return tuple((c0, min(step, d_ff - c0)) for c0 in range(0, d_ff, step))


def _const_spec(shape):
    nd = len(shape)
    return pl.BlockSpec(shape, lambda *_: (0,) * nd, pipeline_mode=pl.Buffered(1))


def _ffn(x, m_spec, m_arr, g, w_up, w_down, tm):
    rows, d = x.shape
    d_ff = w_down.shape[0]
    kern = functools.partial(_ffn_kernel, d_ff=d_ff, chunks=_ffn_chunks(d_ff))
    return pl.pallas_call(
        kern,
        grid=(rows // tm,),
        in_specs=[pl.BlockSpec((tm, d), lambda i: (i, 0)),
                  m_spec,
                  _const_spec((1, d)),
                  _const_spec(w_up.shape),
                  _const_spec(w_down.shape)],
        out_specs=pl.BlockSpec((tm, d), lambda i: (i, 0)),
        out_shape=jax.ShapeDtypeStruct((rows, d), F32),
        compiler_params=pltpu.CompilerParams(dimension_semantics=("arbitrary",),
                                             vmem_limit_bytes=VMEM_LIMIT),
    )(x, m_arr, g.reshape(1, d), w_up, w_down)


class _MixerWeights(NamedTuple):
    g_mix: Any
    w_in: Any
    w_kr: Any
    w_gate: Any
    g_gmlp_v: Any
    gmlp_ws: Any
    gmlp_bias: Any
    g_q_lat: Any
    w_uq: Any
    g_kv_lat: Any
    w_uk: Any
    w_uv: Any
    g_qnorm: Any
    g_knorm: Any
    b_gate: Any
    w_branch_a: Any
    w_branch_b: Any
    w_out: Any


N_MIXER_WEIGHTS = len(_MixerWeights._fields)


def _tokenwise(x, m_ref, cos_ref, sin_ref, w, oa_scr, *, offs, chunk_len):
    o_u, o_v, o_q, o_kv, o_end = offs
    tm, d = x.shape
    shift, scale = m_ref[0], m_ref[1]
    h = (_rms(x, w.g_mix[...]) * (1.0 + scale) + shift).astype(BF16)

    cos = cos_ref[...]
    sin = sin_ref[...]
    qn = _rms(_dot(h, w.w_in[:, o_q:o_kv]), w.g_q_lat[...]).astype(BF16)
    ckv = _rms(_dot(h, w.w_in[:, o_kv:o_end]), w.g_kv_lat[...])
    kr2 = _dot(h, w.w_kr[...])
    kr_slot = kr2[:, :LANES] * cos + kr2[:, LANES:] * sin
    ckvb = ckv.astype(BF16)
    q2 = _dot(qn, w.w_uq[...])
    kn = _dot(ckvb, w.w_uk[...])
    vv = _dot(ckvb, w.w_uv[...])
    gq = w.g_qnorm[...] * (QK_HEAD ** -0.5 * LOG2E)
    gk = w.g_knorm[...]
    hw = HEADS * LANES
    q_heads, k_heads = [], []
    for hd in range(HEADS):
        sl = slice(hd * LANES, (hd + 1) * LANES)
        qh = q2[:, sl] * cos + q2[:, hw + hd * LANES:hw + (hd + 1) * LANES] * sin
        q_heads.append(_head_norm(qh, gq).astype(BF16))
        k_heads.append(_head_norm(kn[:, sl] + kr_slot, gk).astype(BF16))
    v_heads = _v_slots(vv)

    u = jax.nn.gelu(_dot(h, w.w_in[:, o_u:o_v]))
    v = _rms(jax.nn.gelu(_dot(h, w.w_in[:, o_v:o_q])), w.g_gmlp_v[...])
    gates = jax.nn.sigmoid(_dot(h, w.w_gate[...]) + w.b_gate[...])
    vb = v.astype(BF16)
    L = chunk_len
    gw = d // GMLP_GROUPS
    row = lax.broadcasted_iota(jnp.int32, (L, L), 0)
    col = lax.broadcasted_iota(jnp.int32, (L, L), 1)
    tril = col <= row
    bias = w.gmlp_bias[...]
    for g in range(GMLP_GROUPS):
        wg = jnp.where(tril, w.gmlp_ws[g, 0:L, 0:L], 0.0).astype(BF16)
        for c in range(tm // L):
            r0 = c * L
            mixed = _dot(wg, vb[r0:r0 + L, g * gw:(g + 1) * gw]) + bias[:, g * gw:(g + 1) * gw]
            oa_scr[r0:r0 + L, g * gw:(g + 1) * gw] = (
                u[r0:r0 + L, g * gw:(g + 1) * gw] * mixed).astype(BF16)
    merged_a = gates[:, :d] * _dot(oa_scr[...], w.w_branch_a[...])
    gate_b = gates[:, d:]
    return merged_a, gate_b, q_heads, k_heads, v_heads, ckv, kr_slot, v


def _v_slots(vv):
    one = (lax.broadcasted_iota(jnp.int32, (1, LANES), 1) == V_HEAD).astype(F32)
    return [(vv[:, hd * LANES:(hd + 1) * LANES] + one).astype(BF16) for hd in range(HEADS)]


def _normalise(acc):
    return acc / acc[:, V_HEAD:V_HEAD + 1]


def _finish(x, gate, merged_a, gate_b, o_heads, w):
    wb_ref, wo_ref = w.w_branch_b, w.w_out
    tm = x.shape[0]
    lane = lax.broadcasted_iota(jnp.int32, (tm, LANES), 1)
    parts = [jnp.where(lane < V_HEAD, o_heads[2 * p],
                       pltpu.roll(o_heads[2 * p + 1], V_HEAD, axis=1)).astype(BF16)
             for p in range(HEADS // 2)]
    o_b = jnp.concatenate(parts, axis=1)
    merged = merged_a + gate_b * _dot(o_b, wb_ref[...])
    return x + gate * _dot(merged.astype(BF16), wo_ref[...])


def _prompt_mixer_kernel(x_ref, m_ref, cos_ref, sin_ref, *rest, offs, n_tiles):
    w = _MixerWeights(*rest[:N_MIXER_WEIGHTS])
    y_ref, ckv_ref, kr_ref, oa_scr, k_scr, v_scr = rest[N_MIXER_WEIGHTS:]
    t = pl.program_id(1)
    x = x_ref[...]
    tm = x.shape[0]
    merged_a, gate_b, q_heads, k_heads, v_heads, ckv, kr_slot, _ = _tokenwise(
        x, m_ref, cos_ref, sin_ref, w, oa_scr, offs=offs, chunk_len=GMLP_CHUNK)
    ckv_ref[...] = ckv
    kr_ref[...] = kr_slot[:, QK_NOPE:QK_HEAD]
    row0 = pl.multiple_of(t * tm, tm)
    for hd in range(HEADS):
        k_scr[hd, pl.ds(row0, tm), :] = k_heads[hd]
        v_scr[hd, pl.ds(row0, tm), :] = v_heads[hd]

    qrow = lax.broadcasted_iota(jnp.int32, (tm, tm), 0) // CHUNK
    kcol = lax.broadcasted_iota(jnp.int32, (tm, tm), 1) // CHUNK
    diag_mask = kcol <= qrow
    ms, accs = [], []
    for hd in range(HEADS):
        s = jnp.where(diag_mask, _dot_nt(q_heads[hd], k_heads[hd]), NEG)
        m0 = jnp.max(s, axis=1, keepdims=True)
        ms.append(m0)
        accs.append(_dot(jnp.exp2(s - m0).astype(BF16), v_heads[hd]))

    for n_prev in range(n_tiles):
        @pl.when(t == n_prev)
        def _(n_prev=n_prev):
            accs_ = list(accs)
            if n_prev:
                prev = slice(0, n_prev * tm)
                for hd in range(HEADS):
                    sp = _dot_nt(q_heads[hd], k_scr[hd, prev, :])
                    m_new = jnp.maximum(ms[hd], jnp.max(sp, axis=1, keepdims=True))
                    pp = jnp.exp2(sp - m_new).astype(BF16)
                    accs_[hd] = jnp.exp2(ms[hd] - m_new) * accs[hd] + _dot(pp, v_scr[hd, prev, :])
            o_heads = [_normalise(a) for a in accs_]
            y_ref[...] = _finish(x, m_ref[2], merged_a, gate_b, o_heads, w)


def _sample_mixer_kernel(x_ref, m_ref, cos_ref, sin_ref, pckv_ref, pkr_ref, *rest, offs, key_chunk):
    w = _MixerWeights(*rest[:N_MIXER_WEIGHTS])
    y_ref, ckv_ref, kr_ref, gv_out_ref, oa_scr, k_scr, v_scr = rest[N_MIXER_WEIGHTS:]
    x = x_ref[...]
    tm = x.shape[0]
    past = pckv_ref.shape[0]
    merged_a, gate_b, q_heads, k_heads, v_heads, ckv, kr_slot, v_gmlp = _tokenwise(
        x, m_ref, cos_ref, sin_ref, w, oa_scr, offs=offs, chunk_len=min(tm, GMLP_CHUNK))
    ckv_ref[...] = ckv
    kr_ref[...] = kr_slot[:, QK_NOPE:QK_HEAD]
    gv_out_ref[...] = v_gmlp

    gk = w.g_knorm[...]

    def build(i, carry):
        r = pl.multiple_of(i * key_chunk, key_chunk)
        cb = pckv_ref[pl.ds(r, key_chunk), :].astype(BF16)
        kn = _dot(cb, w.w_uk[...])
        vs = _v_slots(_dot(cb, w.w_uv[...]))
        kr = pkr_ref[pl.ds(r, key_chunk), :]
        for hd in range(HEADS):
            kh = kn[:, hd * LANES:(hd + 1) * LANES] + kr
            k_scr[hd, pl.ds(r, key_chunk), :] = _head_norm(kh, gk).astype(BF16)
            v_scr[hd, pl.ds(r, key_chunk), :] = vs[hd]
        return carry

    lax.fori_loop(0, past // key_chunk, build, 0)

    o_heads = []
    for hd in range(HEADS):
        q = q_heads[hd]
        sp = _dot_nt(q, k_scr[hd])
        sn = _dot_nt(q, k_heads[hd])
        m = jnp.maximum(jnp.max(sp, axis=1, keepdims=True), jnp.max(sn, axis=1, keepdims=True))
        pp = jnp.exp2(sp - m).astype(BF16)
        pn = jnp.exp2(sn - m).astype(BF16)
        o_heads.append(_normalise(_dot(pp, v_scr[hd]) + _dot(pn, v_heads[hd])))
    y_ref[...] = _finish(x, m_ref[2], merged_a, gate_b, o_heads, w)


def _mixer_weight_specs(wts):
    return [_const_spec(w.shape) for w in wts]


def _prompt_mixer(x, m_arr, cos, sin, wts, offs, batch, seq, tm, kv_rank):
    rows, d = x.shape
    nt = seq // tm
    kern = functools.partial(_prompt_mixer_kernel, offs=offs, n_tiles=nt)
    row_spec = lambda w: pl.BlockSpec((tm, w), lambda b, t: (b * nt + t, 0))
    return pl.pallas_call(
        kern,
        grid=(batch, nt),
        in_specs=[row_spec(d),
                  pl.BlockSpec((None, 3, None, 1, d), lambda b, t: (1, 0, b, 0, 0)),
                  pl.BlockSpec((tm, LANES), lambda b, t: (t, 0)),
                  pl.BlockSpec((tm, LANES), lambda b, t: (t, 0))] + _mixer_weight_specs(wts),
        out_specs=[row_spec(d), row_spec(kv_rank), row_spec(QK_ROPE)],
        out_shape=[jax.ShapeDtypeStruct((rows, d), F32),
                   jax.ShapeDtypeStruct((rows, kv_rank), F32),
                   jax.ShapeDtypeStruct((rows, QK_ROPE), F32)],
        scratch_shapes=[pltpu.VMEM((tm, d), BF16),
                        pltpu.VMEM((HEADS, seq, LANES), BF16),
                        pltpu.VMEM((HEADS, seq, LANES), BF16)],
        compiler_params=pltpu.CompilerParams(dimension_semantics=("arbitrary", "arbitrary"),
                                             vmem_limit_bytes=VMEM_LIMIT),
    )(x, m_arr, cos, sin, *wts)


def _sample_mixer(x, m_arr, cos, sin, past_ckv, past_kr_slot, wts, offs, batch, seq, kv_rank):
    rows, d = x.shape
    past = past_ckv.shape[1]
    key_chunk = 512 if past % 512 == 0 else past
    kern = functools.partial(_sample_mixer_kernel, offs=offs, key_chunk=key_chunk)
    row_spec = lambda w: pl.BlockSpec((seq, w), lambda b: (b, 0))
    return pl.pallas_call(
        kern,
        grid=(batch,),
        in_specs=[row_spec(d),
                  pl.BlockSpec((None, 3, None, 1, d), lambda b: (1, 0, b, 0, 0)),
                  pl.BlockSpec((seq, LANES), lambda b: (0, 0)),
                  pl.BlockSpec((seq, LANES), lambda b: (0, 0)),
                  pl.BlockSpec((None, past, kv_rank), lambda b: (b, 0, 0)),
                  pl.BlockSpec((None, past, LANES), lambda b: (b, 0, 0))] + _mixer_weight_specs(wts),
        out_specs=[row_spec(d), row_spec(kv_rank), row_spec(QK_ROPE), row_spec(d)],
        out_shape=[jax.ShapeDtypeStruct((rows, d), F32),
                   jax.ShapeDtypeStruct((rows, kv_rank), F32),
                   jax.ShapeDtypeStruct((rows, QK_ROPE), F32),
                   jax.ShapeDtypeStruct((rows, d), F32)],
        scratch_shapes=[pltpu.VMEM((seq, d), BF16),
                        pltpu.VMEM((HEADS, past, LANES), BF16),
                        pltpu.VMEM((HEADS, past, LANES), BF16)],
        compiler_params=pltpu.CompilerParams(dimension_semantics=("arbitrary",),
                                             vmem_limit_bytes=VMEM_LIMIT),
    )(x, m_arr, cos, sin, past_ckv, past_kr_slot, *wts)


def _rope_tables(pos):
    half = QK_ROPE // 2
    freqs = ROPE_THETA ** (-jnp.arange(half, dtype=F32) / half)
    ang = pos[:, None] * freqs[None, :]
    cos, sin = jnp.cos(ang), jnp.sin(ang)
    n = pos.shape[0]
    pad = LANES - QK_HEAD
    c = jnp.concatenate([jnp.ones((n, QK_NOPE), F32), cos, cos, jnp.zeros((n, pad), F32)], axis=1)
    s = jnp.concatenate([jnp.zeros((n, QK_NOPE), F32), -sin, sin, jnp.zeros((n, pad), F32)], axis=1)
    return c, s


def _rope_slot(w):
    half = QK_ROPE // 2
    lead = w.shape[:-1]
    z0 = jnp.zeros(lead + (QK_NOPE,), w.dtype)
    z1 = jnp.zeros(lead + (LANES - QK_HEAD,), w.dtype)
    slot = jnp.concatenate([z0, w, z1], axis=-1)
    swapped = jnp.concatenate([z0, w[..., half:], w[..., :half], z1], axis=-1)
    return slot, swapped


def _layer_weights(g_mix, w_in, g_gmlp_v, gmlp_ws, gmlp_b, g_q_lat, w_uq, g_kv_lat, w_uk, w_uv,
                   g_qnorm, g_knorm, b_gate, w_branch_a, w_branch_b, w_out, chunk_len):
    d = w_in.shape[0]
    d_a = g_gmlp_v.shape[0]
    q_rank = g_q_lat.shape[0]
    kv_rank = g_kv_lat.shape[0]
    offs = _in_offsets(d_a, q_rank, kv_rank)
    o_kr = offs[-1]
    win_lead = w_in[:, :o_kr].astype(BF16)
    kr_slot, kr_swapped = _rope_slot(w_in[:, o_kr:o_kr + QK_ROPE])
    w_kr = jnp.concatenate([kr_slot, kr_swapped], axis=1).astype(BF16)
    w_gate = w_in[:, o_kr + QK_ROPE:].astype(BF16)

    uq3 = w_uq.reshape(q_rank, HEADS, QK_HEAD)
    rope_slot, rope_swapped = _rope_slot(uq3[..., QK_NOPE:])
    nope = jnp.concatenate([uq3[..., :QK_NOPE], jnp.zeros((q_rank, HEADS, LANES - QK_NOPE), F32)], -1)
    wuq_p = jnp.concatenate([(nope + rope_slot).reshape(q_rank, HEADS * LANES),
                             rope_swapped.reshape(q_rank, HEADS * LANES)], axis=1).astype(BF16)
    uk3 = w_uk.reshape(kv_rank, HEADS, QK_NOPE)
    wuk_p = jnp.concatenate([uk3, jnp.zeros((kv_rank, HEADS, LANES - QK_NOPE), F32)], -1)
    wuk_p = wuk_p.reshape(kv_rank, HEADS * LANES).astype(BF16)
    uv3 = w_uv.reshape(kv_rank, HEADS, V_HEAD)
    wuv_p = jnp.concatenate([uv3, jnp.zeros((kv_rank, HEADS, LANES - V_HEAD), F32)], -1)
    wuv_p = wuv_p.reshape(kv_rank, HEADS * LANES).astype(BF16)
    pad = jnp.zeros((LANES - QK_HEAD,), F32)
    gq = jnp.concatenate([g_qnorm, pad]).reshape(1, LANES)
    gk = jnp.concatenate([g_knorm, pad]).reshape(1, LANES)
    gw = d_a // GMLP_GROUPS
    bias = jnp.repeat(gmlp_b[:, :chunk_len].T, gw, axis=1)
    wts = _MixerWeights(
        g_mix=g_mix.reshape(1, d), w_in=win_lead, w_kr=w_kr, w_gate=w_gate,
        g_gmlp_v=g_gmlp_v.reshape(1, d_a), gmlp_ws=gmlp_ws, gmlp_bias=bias,
        g_q_lat=g_q_lat.reshape(1, q_rank), w_uq=wuq_p, g_kv_lat=g_kv_lat.reshape(1, kv_rank),
        w_uk=wuk_p, w_uv=wuv_p, g_qnorm=gq, g_knorm=gk, b_gate=b_gate.reshape(1, -1),
        w_branch_a=w_branch_a.astype(BF16), w_branch_b=w_branch_b.astype(BF16), w_out=w_out.astype(BF16))
    return wts, offs


def _prompt_tile(seq):
    for tm in (512, 256, 128):
        if seq % tm == 0:
            return tm
    raise ValueError("prompt length must be a multiple of 128")


def _ffn_tile(seq):
    for tm in (1024, 512, 256, 128):
        if seq % tm == 0:
            return tm
    raise ValueError("prompt length must be a multiple of 128")


def kernel(x_prompt, x_sample, c_prompt, c_sample, cache_ckv, cache_krope, w_mod, b_mod, g_ffn1, w_ffn1_up, w_ffn1_down, g_mix, w_in, g_gmlp_v, gmlp_ws, gmlp_b, g_q_lat, w_uq, g_kv_lat, w_uk, w_uv, g_qnorm, g_knorm, b_gate, w_branch_a, w_branch_b, w_out, g_ffn2, w_ffn2_up, w_ffn2_down):
    bp, tp, d = x_prompt.shape
    bs, ts, _ = x_sample.shape
    depth = w_mod.shape[0]
    past = cache_ckv.shape[2]
    kv_rank = g_kv_lat.shape[1]
    tm = _prompt_tile(tp)
    assert ts % 8 == 0 and ts <= GMLP_CHUNK and (bs * ts) % 8 == 0

    cos_p, sin_p = _rope_tables(jnp.arange(tp, dtype=F32))
    cos_s, sin_s = _rope_tables(jnp.arange(ts, dtype=F32) + jnp.float32(past))
    c_all = jnp.concatenate([c_prompt, c_sample], axis=0)

    xp = x_prompt.reshape(bp * tp, d)
    xs = x_sample.reshape(bs * ts, d)
    outs = {k: [] for k in ("ckv_p", "kr_p", "ckv_s", "kr_s", "vg_s")}
    for l in range(depth):
        m = _modulation(c_all, w_mod[l], b_mod[l])
        m5 = m.reshape(bp + bs, 3, 3, d).transpose(1, 2, 0, 3)
        m_p = m5[:, :, :bp].reshape(3, 3, bp, 1, d)
        m_s = m5[:, :, bp:].reshape(3, 3, bs, 1, d)
        m_s_rows = jnp.repeat(m5[:, :, bp:], ts, axis=2)
        nt = tp // tm
        rows_s = bs * ts

        tm_ffn = _ffn_tile(tp)
        nt_ffn = tp // tm_ffn

        def ffn_pair(sub, g, w_up, w_down, xp, xs):
            w_up_b, w_down_b = w_up.astype(BF16), w_down.astype(BF16)
            p_spec = pl.BlockSpec((None, 3, None, 1, d), lambda i: (sub, 0, i // nt_ffn, 0, 0))
            s_spec = pl.BlockSpec((None, 3, rows_s, d), lambda i: (sub, 0, 0, 0))
            return (_ffn(xp, p_spec, m_p, g, w_up_b, w_down_b, tm_ffn),
                    _ffn(xs, s_spec, m_s_rows, g, w_up_b, w_down_b, rows_s))

        xp, xs = ffn_pair(0, g_ffn1[l], w_ffn1_up[l], w_ffn1_down[l], xp, xs)

        layer_args = (g_mix[l], w_in[l], g_gmlp_v[l], gmlp_ws[l], gmlp_b[l], g_q_lat[l], w_uq[l],
                      g_kv_lat[l], w_uk[l], w_uv[l], g_qnorm[l], g_knorm[l], b_gate[l],
                      w_branch_a[l], w_branch_b[l], w_out[l])
        wts_p, offs = _layer_weights(*layer_args, chunk_len=GMLP_CHUNK)
        xp, ckv_p, kr_p = _prompt_mixer(xp, m_p, cos_p, sin_p, wts_p, offs, bp, tp, tm, kv_rank)
        wts_s, _ = _layer_weights(*layer_args, chunk_len=min(ts, GMLP_CHUNK))
        past_kr_slot, _ = _rope_slot(cache_krope[l])
        xs, ckv_s, kr_s, vg_s = _sample_mixer(xs, m_s, cos_s, sin_s, cache_ckv[l], past_kr_slot,
                                              wts_s, offs, bs, ts, kv_rank)

        xp, xs = ffn_pair(2, g_ffn2[l], w_ffn2_up[l], w_ffn2_down[l], xp, xs)

        outs["ckv_p"].append(ckv_p.reshape(bp, tp, kv_rank))
        outs["kr_p"].append(kr_p.reshape(bp, tp, QK_ROPE))
        outs["ckv_s"].append(ckv_s.reshape(bs, ts, kv_rank))
        outs["kr_s"].append(kr_s.reshape(bs, ts, QK_ROPE))
        outs["vg_s"].append(vg_s.reshape(bs, ts, d))
    return (xp.reshape(bp, tp, d), xs.reshape(bs, ts, d),
            jnp.stack(outs["ckv_p"], 0), jnp.stack(outs["kr_p"], 0),
            jnp.stack(outs["ckv_s"], 0), jnp.stack(outs["kr_s"], 0), jnp.stack(outs["vg_s"], 0))
```

```python
import functools
from typing import Any, NamedTuple

import jax
import jax.numpy as jnp
from jax import lax
from jax.experimental import pallas as pl
from jax.experimental.pallas import tpu as pltpu

F32 = jnp.float32
BF16 = jnp.bfloat16

EPS = 1e-6
NEG = -1e30
ROPE_THETA = 10000.0
N_MOD = 9
CHUNK = 64
GMLP_CHUNK = 128
GMLP_GROUPS = 8
HEADS = 8
QK_NOPE = 64
QK_ROPE = 32
QK_HEAD = QK_NOPE + QK_ROPE
V_HEAD = 64
LANES = 128
VMEM_LIMIT = 60 * 1024 * 1024
LOG2E = 1.4426950408889634


def _in_offsets(d_a, q_rank, kv_rank):
    o_u = 0
    o_v = o_u + d_a
    o_q = o_v + d_a
    o_kv = o_q + q_rank
    o_end = o_kv + kv_rank
    return o_u, o_v, o_q, o_kv, o_end


def _dot(a, b):
    return jnp.dot(a, b, preferred_element_type=F32)


def _dot_nt(a, b):
    return lax.dot_general(a, b, (((1,), (1,)), ((), ())), preferred_element_type=F32)


def _rms(x, g):
    ms = jnp.mean(x * x, axis=-1, keepdims=True)
    return x * lax.rsqrt(ms + EPS) * g


def _modulate(x, g, shift, scale):
    ms = jnp.mean(x * x, axis=-1, keepdims=True)
    xn = (x * lax.rsqrt(ms + EPS)).astype(BF16)
    return xn * (g * (1.0 + scale)).astype(BF16) + shift.astype(BF16)


def _head_norm(x, g):
    ms = jnp.sum(x * x, axis=-1, keepdims=True) * (1.0 / QK_HEAD)
    return x * lax.rsqrt(ms + EPS) * g


def _mod_kernel(c_ref, w_ref, b_ref, o_ref):
    c = c_ref[...]
    a = (c * jax.nn.sigmoid(c)).astype(BF16)
    o_ref[...] = _dot(a, w_ref[...].astype(BF16)) + b_ref[...]


def _modulation(c, w_mod, b_mod):
    n, d = c.shape
    nout = w_mod.shape[1]
    bn = nout // 8
    return pl.pallas_call(
        _mod_kernel,
        grid=(nout // bn,),
        in_specs=[pl.BlockSpec((n, d), lambda j: (0, 0)),
                  pl.BlockSpec((d, bn), lambda j: (0, j)),
                  pl.BlockSpec((1, bn), lambda j: (0, j))],
        out_specs=pl.BlockSpec((n, bn), lambda j: (0, j)),
        out_shape=jax.ShapeDtypeStruct((n, nout), F32),
        compiler_params=pltpu.CompilerParams(dimension_semantics=("arbitrary",),
                                             vmem_limit_bytes=VMEM_LIMIT),
    )(c, w_mod, b_mod.reshape(1, nout))


def _ffn_kernel(x_ref, m_ref, g_ref, wup_ref, wdn_ref, o_ref, *, d_ff, chunks):
    x = x_ref[...]
    shift, scale, gate = m_ref[0], m_ref[1], m_ref[2]
    h = _modulate(x, g_ref[...], shift, scale)
    acc = None
    for c0, cw in chunks:
        a = _dot(h, wup_ref[:, c0:c0 + cw])
        b = _dot(h, wup_ref[:, d_ff + c0:d_ff + c0 + cw])
        act = (a * jax.nn.sigmoid(a) * b).astype(BF16)
        part = _dot(act, wdn_ref[c0:c0 + cw, :])
        acc = part if acc is None else acc + part
    o_ref[...] = x + (0.5 * gate) * acc


def _ffn_chunks(d_ff):
    step = 1024
    return tuple((c0, min(step, d_ff - c0)) for c0 in range(0, d_ff, step))


def _const_spec(shape):
    nd = len(shape)
    return pl.BlockSpec(shape, lambda *_: (0,) * nd, pipeline_mode=pl.Buffered(1))


def _ffn(x, m_spec, m_arr, g, w_up, w_down, tm):
    rows, d = x.shape
    d_ff = w_down.shape[0]
    kern = functools.partial(_ffn_kernel, d_ff=d_ff, chunks=_ffn_chunks(d_ff))
    return pl.pallas_call(
        kern,
        grid=(rows // tm,),
        in_specs=[pl.BlockSpec((tm, d), lambda i: (i, 0)),
                  m_spec,
                  _const_spec((1, d)),
                  _const_spec(w_up.shape),
                  _const_spec(w_down.shape)],
        out_specs=pl.BlockSpec((tm, d), lambda i: (i, 0)),
        out_shape=jax.ShapeDtypeStruct((rows, d), F32),
        compiler_params=pltpu.CompilerParams(dimension_semantics=("arbitrary",),
                                             vmem_limit_bytes=VMEM_LIMIT),
    )(x, m_arr, g.reshape(1, d), w_up, w_down)


class _MixerWeights(NamedTuple):
    g_mix: Any
    w_in: Any
    w_kr: Any
    w_gate: Any
    g_gmlp_v: Any
    gmlp_ws: Any
    gmlp_bias: Any
    g_q_lat: Any
    w_uq: Any
    g_kv_lat: Any
    w_uk: Any
    w_uv: Any
    g_qnorm: Any
    g_knorm: Any
    b_gate: Any
    w_branch_a: Any
    w_branch_b: Any
    w_out: Any


N_MIXER_WEIGHTS = len(_MixerWeights._fields)


def _tokenwise(x, m_ref, cos_ref, sin_ref, w, oa_scr, *, offs, chunk_len):
    o_u, o_v, o_q, o_kv, o_end = offs
    tm, d = x.shape
    shift, scale = m_ref[0], m_ref[1]
    h = _modulate(x, w.g_mix[...], shift, scale)

    cos = cos_ref[...]
    sin = sin_ref[...]
    qn = _rms(_dot(h, w.w_in[:, o_q:o_kv]), w.g_q_lat[...]).astype(BF16)
    ckv = _rms(_dot(h, w.w_in[:, o_kv:o_end]), w.g_kv_lat[...])
    kr2 = _dot(h, w.w_kr[...])
    kr_slot = kr2[:, :LANES] * cos + kr2[:, LANES:] * sin
    ckvb = ckv.astype(BF16)
    q2 = _dot(qn, w.w_uq[...])
    kn = _dot(ckvb, w.w_uk[...])
    vv = _dot(ckvb, w.w_uv[...])
    gq = w.g_qnorm[...] * (QK_HEAD ** -0.5 * LOG2E)
    gk = w.g_knorm[...]
    hw = HEADS * LANES
    q_heads, k_heads = [], []
    for hd in range(HEADS):
        sl = slice(hd * LANES, (hd + 1) * LANES)
        qh = q2[:, sl] * cos + q2[:, hw + hd * LANES:hw + (hd + 1) * LANES] * sin
        q_heads.append(_head_norm(qh, gq).astype(BF16))
        k_heads.append(_head_norm(kn[:, sl] + kr_slot, gk).astype(BF16))
    v_heads = _v_slots(vv)

    u = jax.nn.gelu(_dot(h, w.w_in[:, o_u:o_v]))
    v = _rms(jax.nn.gelu(_dot(h, w.w_in[:, o_v:o_q])), w.g_gmlp_v[...])
    gates = jax.nn.sigmoid(_dot(h, w.w_gate[...]) + w.b_gate[...])
    vb = v.astype(BF16)
    L = chunk_len
    gw = d // GMLP_GROUPS
    row = lax.broadcasted_iota(jnp.int32, (L, L), 0)
    col = lax.broadcasted_iota(jnp.int32, (L, L), 1)
    tril = col <= row
    bias = w.gmlp_bias[...]
    for g in range(GMLP_GROUPS):
        wg = jnp.where(tril, w.gmlp_ws[g, 0:L, 0:L], 0.0).astype(BF16)
        for c in range(tm // L):
            r0 = c * L
            mixed = _dot(wg, vb[r0:r0 + L, g * gw:(g + 1) * gw]) + bias[:, g * gw:(g + 1) * gw]
            oa_scr[r0:r0 + L, g * gw:(g + 1) * gw] = (
                u[r0:r0 + L, g * gw:(g + 1) * gw] * mixed).astype(BF16)
    merged_a = gates[:, :d] * _dot(oa_scr[...], w.w_branch_a[...])
    gate_b = gates[:, d:]
    return merged_a, gate_b, q_heads, k_heads, v_heads, ckv, kr_slot, v


def _v_slots(vv):
    one = (lax.broadcasted_iota(jnp.int32, (1, LANES), 1) == V_HEAD).astype(F32)
    return [(vv[:, hd * LANES:(hd + 1) * LANES] + one).astype(BF16) for hd in range(HEADS)]


def _normalise(acc):
    return acc / acc[:, V_HEAD:V_HEAD + 1]


def _finish(x, gate, merged_a, gate_b, o_heads, w):
    wb_ref, wo_ref = w.w_branch_b, w.w_out
    tm = x.shape[0]
    lane = lax.broadcasted_iota(jnp.int32, (tm, LANES), 1)
    parts = [jnp.where(lane < V_HEAD, o_heads[2 * p],
                       pltpu.roll(o_heads[2 * p + 1], V_HEAD, axis=1)).astype(BF16)
             for p in range(HEADS // 2)]
    o_b = jnp.concatenate(parts, axis=1)
    merged = merged_a + gate_b * _dot(o_b, wb_ref[...])
    return x + gate * _dot(merged.astype(BF16), wo_ref[...])


def _prompt_mixer_kernel(x_ref, m_ref, cos_ref, sin_ref, *rest, offs, n_tiles):
    w = _MixerWeights(*rest[:N_MIXER_WEIGHTS])
    y_ref, ckv_ref, kr_ref, oa_scr, k_scr, v_scr = rest[N_MIXER_WEIGHTS:]
    t = pl.program_id(1)
    x = x_ref[...]
    tm = x.shape[0]
    merged_a, gate_b, q_heads, k_heads, v_heads, ckv, kr_slot, _ = _tokenwise(
        x, m_ref, cos_ref, sin_ref, w, oa_scr, offs=offs, chunk_len=GMLP_CHUNK)
    ckv_ref[...] = ckv
    kr_ref[...] = kr_slot[:, QK_NOPE:QK_HEAD]
    row0 = pl.multiple_of(t * tm, tm)
    for hd in range(HEADS):
        k_scr[hd, pl.ds(row0, tm), :] = k_heads[hd]
        v_scr[hd, pl.ds(row0, tm), :] = v_heads[hd]

    qrow = lax.broadcasted_iota(jnp.int32, (tm, tm), 0) // CHUNK
    kcol = lax.broadcasted_iota(jnp.int32, (tm, tm), 1) // CHUNK
    diag_mask = kcol <= qrow
    ms, accs = [], []
    for hd in range(HEADS):
        s = jnp.where(diag_mask, _dot_nt(q_heads[hd], k_heads[hd]), NEG)
        m0 = jnp.max(s, axis=1, keepdims=True)
        ms.append(m0)
        accs.append(_dot(jnp.exp2(s - m0).astype(BF16), v_heads[hd]))

    for n_prev in range(n_tiles):
        @pl.when(t == n_prev)
        def _(n_prev=n_prev):
            accs_ = list(accs)
            if n_prev:
                prev = slice(0, n_prev * tm)
                for hd in range(HEADS):
                    sp = _dot_nt(q_heads[hd], k_scr[hd, prev, :])
                    m_new = jnp.maximum(ms[hd], jnp.max(sp, axis=1, keepdims=True))
                    pp = jnp.exp2(sp - m_new).astype(BF16)
                    accs_[hd] = jnp.exp2(ms[hd] - m_new) * accs[hd] + _dot(pp, v_scr[hd, prev, :])
            o_heads = [_normalise(a) for a in accs_]
            y_ref[...] = _finish(x, m_ref[2], merged_a, gate_b, o_heads, w)


def _sample_mixer_kernel(x_ref, m_ref, cos_ref, sin_ref, pckv_ref, pkr_ref, *rest, offs, key_chunk):
    w = _MixerWeights(*rest[:N_MIXER_WEIGHTS])
    y_ref, ckv_ref, kr_ref, gv_out_ref, oa_scr, k_scr, v_scr = rest[N_MIXER_WEIGHTS:]
    x = x_ref[...]
    tm = x.shape[0]
    past = pckv_ref.shape[0]
    merged_a, gate_b, q_heads, k_heads, v_heads, ckv, kr_slot, v_gmlp = _tokenwise(
        x, m_ref, cos_ref, sin_ref, w, oa_scr, offs=offs, chunk_len=min(tm, GMLP_CHUNK))
    ckv_ref[...] = ckv
    kr_ref[...] = kr_slot[:, QK_NOPE:QK_HEAD]
    gv_out_ref[...] = v_gmlp

    gk = w.g_knorm[...]

    def build(i, carry):
        r = pl.multiple_of(i * key_chunk, key_chunk)
        cb = pckv_ref[pl.ds(r, key_chunk), :].astype(BF16)
        kn = _dot(cb, w.w_uk[...])
        vs = _v_slots(_dot(cb, w.w_uv[...]))
        kr = pkr_ref[pl.ds(r, key_chunk), :]
        for hd in range(HEADS):
            kh = kn[:, hd * LANES:(hd + 1) * LANES] + kr
            k_scr[hd, pl.ds(r, key_chunk), :] = _head_norm(kh, gk).astype(BF16)
            v_scr[hd, pl.ds(r, key_chunk), :] = vs[hd]
        return carry

    lax.fori_loop(0, past // key_chunk, build, 0)

    o_heads = []
    for hd in range(HEADS):
        q = q_heads[hd]
        sp = _dot_nt(q, k_scr[hd])
        sn = _dot_nt(q, k_heads[hd])
        m = jnp.maximum(jnp.max(sp, axis=1, keepdims=True), jnp.max(sn, axis=1, keepdims=True))
        pp = jnp.exp2(sp - m).astype(BF16)
        pn = jnp.exp2(sn - m).astype(BF16)
        o_heads.append(_normalise(_dot(pp, v_scr[hd]) + _dot(pn, v_heads[hd])))
    y_ref[...] = _finish(x, m_ref[2], merged_a, gate_b, o_heads, w)


def _mixer_weight_specs(wts):
    return [_const_spec(w.shape) for w in wts]


def _prompt_mixer(x, m_arr, cos, sin, wts, offs, batch, seq, tm, kv_rank):
    rows, d = x.shape
    nt = seq // tm
    kern = functools.partial(_prompt_mixer_kernel, offs=offs, n_tiles=nt)
    row_spec = lambda w: pl.BlockSpec((tm, w), lambda b, t: (b * nt + t, 0))
    return pl.pallas_call(
        kern,
        grid=(batch, nt),
        in_specs=[row_spec(d),
                  pl.BlockSpec((None, 3, None, 1, d), lambda b, t: (1, 0, b, 0, 0)),
                  pl.BlockSpec((tm, LANES), lambda b, t: (t, 0)),
                  pl.BlockSpec((tm, LANES), lambda b, t: (t, 0))] + _mixer_weight_specs(wts),
        out_specs=[row_spec(d), row_spec(kv_rank), row_spec(QK_ROPE)],
        out_shape=[jax.ShapeDtypeStruct((rows, d), F32),
                   jax.ShapeDtypeStruct((rows, kv_rank), F32),
                   jax.ShapeDtypeStruct((rows, QK_ROPE), F32)],
        scratch_shapes=[pltpu.VMEM((tm, d), BF16),
                        pltpu.VMEM((HEADS, seq, LANES), BF16),
                        pltpu.VMEM((HEADS, seq, LANES), BF16)],
        compiler_params=pltpu.CompilerParams(dimension_semantics=("arbitrary", "arbitrary"),
                                             vmem_limit_bytes=VMEM_LIMIT),
    )(x, m_arr, cos, sin, *wts)


def _sample_mixer(x, m_arr, cos, sin, past_ckv, past_kr_slot, wts, offs, batch, seq, kv_rank):
    rows, d = x.shape
    past = past_ckv.shape[1]
    key_chunk = 1024 if past % 1024 == 0 else past
    kern = functools.partial(_sample_mixer_kernel, offs=offs, key_chunk=key_chunk)
    row_spec = lambda w: pl.BlockSpec((seq, w), lambda b: (b, 0))
    return pl.pallas_call(
        kern,
        grid=(batch,),
        in_specs=[row_spec(d),
                  pl.BlockSpec((None, 3, None, 1, d), lambda b: (1, 0, b, 0, 0)),
                  pl.BlockSpec((seq, LANES), lambda b: (0, 0)),
                  pl.BlockSpec((seq, LANES), lambda b: (0, 0)),
                  pl.BlockSpec((None, past, kv_rank), lambda b: (b, 0, 0)),
                  pl.BlockSpec((None, past, LANES), lambda b: (b, 0, 0))] + _mixer_weight_specs(wts),
        out_specs=[row_spec(d), row_spec(kv_rank), row_spec(QK_ROPE), row_spec(d)],
        out_shape=[jax.ShapeDtypeStruct((rows, d), F32),
                   jax.ShapeDtypeStruct((rows, kv_rank), F32),
                   jax.ShapeDtypeStruct((rows, QK_ROPE), F32),
                   jax.ShapeDtypeStruct((rows, d), F32)],
        scratch_shapes=[pltpu.VMEM((seq, d), BF16),
                        pltpu.VMEM((HEADS, past, LANES), BF16),
                        pltpu.VMEM((HEADS, past, LANES), BF16)],
        compiler_params=pltpu.CompilerParams(dimension_semantics=("arbitrary",),
                                             vmem_limit_bytes=VMEM_LIMIT),
    )(x, m_arr, cos, sin, past_ckv, past_kr_slot, *wts)


def _rope_tables(pos):
    half = QK_ROPE // 2
    freqs = ROPE_THETA ** (-jnp.arange(half, dtype=F32) / half)
    ang = pos[:, None] * freqs[None, :]
    cos, sin = jnp.cos(ang), jnp.sin(ang)
    n = pos.shape[0]
    pad = LANES - QK_HEAD
    c = jnp.concatenate([jnp.ones((n, QK_NOPE), F32), cos, cos, jnp.zeros((n, pad), F32)], axis=1)
    s = jnp.concatenate([jnp.zeros((n, QK_NOPE), F32), -sin, sin, jnp.zeros((n, pad), F32)], axis=1)
    return c, s


def _rope_slot(w):
    half = QK_ROPE // 2
    lead = w.shape[:-1]
    z0 = jnp.zeros(lead + (QK_NOPE,), w.dtype)
    z1 = jnp.zeros(lead + (LANES - QK_HEAD,), w.dtype)
    slot = jnp.concatenate([z0, w, z1], axis=-1)
    swapped = jnp.concatenate([z0, w[..., half:], w[..., :half], z1], axis=-1)
    return slot, swapped


def _layer_weights(g_mix, w_in, g_gmlp_v, gmlp_ws, gmlp_b, g_q_lat, w_uq, g_kv_lat, w_uk, w_uv,
                   g_qnorm, g_knorm, b_gate, w_branch_a, w_branch_b, w_out, chunk_len):
    d = w_in.shape[0]
    d_a = g_gmlp_v.shape[0]
    q_rank = g_q_lat.shape[0]
    kv_rank = g_kv_lat.shape[0]
    offs = _in_offsets(d_a, q_rank, kv_rank)
    o_kr = offs[-1]
    win_lead = w_in[:, :o_kr].astype(BF16)
    kr_slot, kr_swapped = _rope_slot(w_in[:, o_kr:o_kr + QK_ROPE])
    w_kr = jnp.concatenate([kr_slot, kr_swapped], axis=1).astype(BF16)
    w_gate = w_in[:, o_kr + QK_ROPE:].astype(BF16)

    uq3 = w_uq.reshape(q_rank, HEADS, QK_HEAD)
    rope_slot, rope_swapped = _rope_slot(uq3[..., QK_NOPE:])
    nope = jnp.concatenate([uq3[..., :QK_NOPE], jnp.zeros((q_rank, HEADS, LANES - QK_NOPE), F32)], -1)
    wuq_p = jnp.concatenate([(nope + rope_slot).reshape(q_rank, HEADS * LANES),
                             rope_swapped.reshape(q_rank, HEADS * LANES)], axis=1).astype(BF16)
    uk3 = w_uk.reshape(kv_rank, HEADS, QK_NOPE)
    wuk_p = jnp.concatenate([uk3, jnp.zeros((kv_rank, HEADS, LANES - QK_NOPE), F32)], -1)
    wuk_p = wuk_p.reshape(kv_rank, HEADS * LANES).astype(BF16)
    uv3 = w_uv.reshape(kv_rank, HEADS, V_HEAD)
    wuv_p = jnp.concatenate([uv3, jnp.zeros((kv_rank, HEADS, LANES - V_HEAD), F32)], -1)
    wuv_p = wuv_p.reshape(kv_rank, HEADS * LANES).astype(BF16)
    pad = jnp.zeros((LANES - QK_HEAD,), F32)
    gq = jnp.concatenate([g_qnorm, pad]).reshape(1, LANES)
    gk = jnp.concatenate([g_knorm, pad]).reshape(1, LANES)
    gw = d_a // GMLP_GROUPS
    bias = jnp.repeat(gmlp_b[:, :chunk_len].T, gw, axis=1)
    wts = _MixerWeights(
        g_mix=g_mix.reshape(1, d), w_in=win_lead, w_kr=w_kr, w_gate=w_gate,
        g_gmlp_v=g_gmlp_v.reshape(1, d_a), gmlp_ws=gmlp_ws, gmlp_bias=bias,
        g_q_lat=g_q_lat.reshape(1, q_rank), w_uq=wuq_p, g_kv_lat=g_kv_lat.reshape(1, kv_rank),
        w_uk=wuk_p, w_uv=wuv_p, g_qnorm=gq, g_knorm=gk, b_gate=b_gate.reshape(1, -1),
        w_branch_a=w_branch_a.astype(BF16), w_branch_b=w_branch_b.astype(BF16), w_out=w_out.astype(BF16))
    return wts, offs


def _prompt_tile(seq):
    for tm in (512, 256, 128):
        if seq % tm == 0:
            return tm
    raise ValueError("prompt length must be a multiple of 128")


def _ffn_tile(seq):
    for tm in (1024, 512, 256, 128):
        if seq % tm == 0:
            return tm
    raise ValueError("prompt length must be a multiple of 128")


def kernel(x_prompt, x_sample, c_prompt, c_sample, cache_ckv, cache_krope, w_mod, b_mod, g_ffn1, w_ffn1_up, w_ffn1_down, g_mix, w_in, g_gmlp_v, gmlp_ws, gmlp_b, g_q_lat, w_uq, g_kv_lat, w_uk, w_uv, g_qnorm, g_knorm, b_gate, w_branch_a, w_branch_b, w_out, g_ffn2, w_ffn2_up, w_ffn2_down):
    bp, tp, d = x_prompt.shape
    bs, ts, _ = x_sample.shape
    depth = w_mod.shape[0]
    past = cache_ckv.shape[2]
    kv_rank = g_kv_lat.shape[1]
    tm = _prompt_tile(tp)
    assert ts % 8 == 0 and ts <= GMLP_CHUNK and (bs * ts) % 8 == 0

    cos_p, sin_p = _rope_tables(jnp.arange(tp, dtype=F32))
    cos_s, sin_s = _rope_tables(jnp.arange(ts, dtype=F32) + jnp.float32(past))
    c_all = jnp.concatenate([c_prompt, c_sample], axis=0)

    xp = x_prompt.reshape(bp * tp, d)
    xs = x_sample.reshape(bs * ts, d)
    outs = {k: [] for k in ("ckv_p", "kr_p", "ckv_s", "kr_s", "vg_s")}
    for l in range(depth):
        m = _modulation(c_all, w_mod[l], b_mod[l])
        m5 = m.reshape(bp + bs, 3, 3, d).transpose(1, 2, 0, 3)
        m_p = m5[:, :, :bp].reshape(3, 3, bp, 1, d)
        m_s = m5[:, :, bp:].reshape(3, 3, bs, 1, d)
        m_s_rows = jnp.repeat(m5[:, :, bp:], ts, axis=2)
        nt = tp // tm
        rows_s = bs * ts

        tm_ffn = _ffn_tile(tp)
        nt_ffn = tp // tm_ffn

        def ffn_pair(sub, g, w_up, w_down, xp, xs):
            w_up_b, w_down_b = w_up.astype(BF16), w_down.astype(BF16)
            p_spec = pl.BlockSpec((None, 3, None, 1, d), lambda i: (sub, 0, i // nt_ffn, 0, 0))
            s_spec = pl.BlockSpec((None, 3, rows_s, d), lambda i: (sub, 0, 0, 0))
            return (_ffn(xp, p_spec, m_p, g, w_up_b, w_down_b, tm_ffn),
                    _ffn(xs, s_spec, m_s_rows, g, w_up_b, w_down_b, rows_s))

        xp, xs = ffn_pair(0, g_ffn1[l], w_ffn1_up[l], w_ffn1_down[l], xp, xs)

        layer_args = (g_mix[l], w_in[l], g_gmlp_v[l], gmlp_ws[l], gmlp_b[l], g_q_lat[l], w_uq[l],
                      g_kv_lat[l], w_uk[l], w_uv[l], g_qnorm[l], g_knorm[l], b_gate[l],
                      w_branch_a[l], w_branch_b[l], w_out[l])
        wts_p, offs = _layer_weights(*layer_args, chunk_len=GMLP_CHUNK)
        xp, ckv_p, kr_p = _prompt_mixer(xp, m_p, cos_p, sin_p, wts_p, offs, bp, tp, tm, kv_rank)
        wts_s, _ = _layer_weights(*layer_args, chunk_len=min(ts, GMLP_CHUNK))
        past_kr_slot, _ = _rope_slot(cache_krope[l])
        xs, ckv_s, kr_s, vg_s = _sample_mixer(xs, m_s, cos_s, sin_s, cache_ckv[l], past_kr_slot,
                                              wts_s, offs, bs, ts, kv_rank)

        xp, xs = ffn_pair(2, g_ffn2[l], w_ffn2_up[l], w_ffn2_down[l], xp, xs)

        outs["ckv_p"].append(ckv_p.reshape(bp, tp, kv_rank))
        outs["kr_p"].append(kr_p.reshape(bp, tp, QK_ROPE))
        outs["ckv_s"].append(ckv_s.reshape(bs, ts, kv_rank))
        outs["kr_s"].append(kr_s.reshape(bs, ts, QK_ROPE))
        outs["vg_s"].append(vg_s.reshape(bs, ts, d))
    return (xp.reshape(bp, tp, d), xs.reshape(bs, ts, d),
            jnp.stack(outs["ckv_p"], 0), jnp.stack(outs["kr_p"], 0),
            jnp.stack(outs["ckv_s"], 0), jnp.stack(outs["kr_s"], 0), jnp.stack(outs["vg_s"], 0))
```

```python
import functools
from typing import Any, NamedTuple

import jax
import jax.numpy as jnp
from jax import lax
from jax.experimental import pallas as pl
from jax.experimental.pallas import tpu as pltpu

F32 = jnp.float32
BF16 = jnp.bfloat16

EPS = 1e-6
NEG = -1e30
ROPE_THETA = 10000.0
N_MOD = 9
CHUNK = 64
GMLP_CHUNK = 128
GMLP_GROUPS = 8
HEADS = 8
QK_NOPE = 64
QK_ROPE = 32
QK_HEAD = QK_NOPE + QK_ROPE
V_HEAD = 64
LANES = 128
VMEM_LIMIT = 60 * 1024 * 1024
LOG2E = 1.4426950408889634


def _in_offsets(d_a, q_rank, kv_rank):
    o_u = 0
    o_v = o_u + d_a
    o_q = o_v + d_a
    o_kv = o_q + q_rank
    o_end = o_kv + kv_rank
    return o_u, o_v, o_q, o_kv, o_end


def _dot(a, b):
    return jnp.dot(a, b, preferred_element_type=F32)


def _dot_nt(a, b):
    return lax.dot_general(a, b, (((1,), (1,)), ((), ())), preferred_element_type=F32)


def _rms(x, g):
    ms = jnp.mean(x * x, axis=-1, keepdims=True)
    return x * lax.rsqrt(ms + EPS) * g


def _modulate(x, g, shift, scale):
    ms = jnp.mean(x * x, axis=-1, keepdims=True)
    xn = (x * lax.rsqrt(ms + EPS)).astype(BF16)
    return xn * (g * (1.0 + scale)).astype(BF16) + shift.astype(BF16)


def _head_norm(x, g):
    ms = jnp.sum(x * x, axis=-1, keepdims=True) * (1.0 / QK_HEAD)
    return x * lax.rsqrt(ms + EPS) * g


def _mod_kernel(c_ref, w_ref, b_ref, o_ref):
    c = c_ref[...]
    a = (c * jax.nn.sigmoid(c)).astype(BF16)
    o_ref[...] = _dot(a, w_ref[...].astype(BF16)) + b_ref[...]


def _modulation(c, w_mod, b_mod):
    n, d = c.shape
    nout = w_mod.shape[1]
    bn = nout // 8
    return pl.pallas_call(
        _mod_kernel,
        grid=(nout // bn,),
        in_specs=[pl.BlockSpec((n, d), lambda j: (0, 0)),
                  pl.BlockSpec((d, bn), lambda j: (0, j)),
                  pl.BlockSpec((1, bn), lambda j: (0, j))],
        out_specs=pl.BlockSpec((n, bn), lambda j: (0, j)),
        out_shape=jax.ShapeDtypeStruct((n, nout), F32),
        compiler_params=pltpu.CompilerParams(dimension_semantics=("arbitrary",),
                                             vmem_limit_bytes=VMEM_LIMIT),
    )(c, w_mod, b_mod.reshape(1, nout))


def _ffn_kernel(x_ref, m_ref, g_ref, wup_ref, wdn_ref, o_ref, *, d_ff, chunks):
    x = x_ref[...]
    shift, scale, gate = m_ref[0], m_ref[1], m_ref[2]
    h = _modulate(x, g_ref[...], shift, scale)
    acc = None
    for c0, cw in chunks:
        a = _dot(h, wup_ref[:, c0:c0 + cw])
        b = _dot(h, wup_ref[:, d_ff + c0:d_ff + c0 + cw])
        act = (a * jax.nn.sigmoid(a) * b).astype(BF16)
        part = _dot(act, wdn_ref[c0:c0 + cw, :])
        acc = part if acc is None else acc + part
    o_ref[...] = x + (0.5 * gate) * acc


def _ffn_chunks(d_ff):
    step = 1024
    return tuple((c0, min(step, d_ff - c0)) for c0 in range(0, d_ff, step))


def _const_spec(shape):
    nd = len(shape)
    return pl.BlockSpec(shape, lambda *_: (0,) * nd, pipeline_mode=pl.Buffered(1))


def _ffn(x, m_spec, m_arr, g, w_up, w_down, tm):
    rows, d = x.shape
    d_ff = w_down.shape[0]
    kern = functools.partial(_ffn_kernel, d_ff=d_ff, chunks=_ffn_chunks(d_ff))
    return pl.pallas_call(
        kern,
        grid=(rows // tm,),
        in_specs=[pl.BlockSpec((tm, d), lambda i: (i, 0)),
                  m_spec,
                  _const_spec((1, d)),
                  _const_spec(w_up.shape),
                  _const_spec(w_down.shape)],
        out_specs=pl.BlockSpec((tm, d), lambda i: (i, 0)),
        out_shape=jax.ShapeDtypeStruct((rows, d), F32),
        compiler_params=pltpu.CompilerParams(dimension_semantics=("arbitrary",),
                                             vmem_limit_bytes=VMEM_LIMIT),
    )(x, m_arr, g.reshape(1, d), w_up, w_down)


class _MixerWeights(NamedTuple):
    g_mix: Any
    w_in: Any
    w_kr: Any
    w_gate: Any
    g_gmlp_v: Any
    gmlp_ws: Any
    gmlp_bias: Any
    g_q_lat: Any
    w_uq: Any
    g_kv_lat: Any
    w_uk: Any
    w_uv: Any
    g_qnorm: Any
    g_knorm: Any
    b_gate: Any
    w_branch_a: Any
    w_branch_b: Any
    w_out: Any


N_MIXER_WEIGHTS = len(_MixerWeights._fields)


def _tokenwise(x, m_ref, cos_ref, sin_ref, w, oa_scr, *, offs, chunk_len):
    o_u, o_v, o_q, o_kv, o_end = offs
    tm, d = x.shape
    shift, scale = m_ref[0], m_ref[1]
    h = _modulate(x, w.g_mix[...], shift, scale)

    cos = cos_ref[...]
    sin = sin_ref[...]
    qn = _rms(_dot(h, w.w_in[:, o_q:o_kv]), w.g_q_lat[...]).astype(BF16)
    ckv = _rms(_dot(h, w.w_in[:, o_kv:o_end]), w.g_kv_lat[...])
    kr2 = _dot(h, w.w_kr[...])
    kr_slot = kr2[:, :LANES] * cos + kr2[:, LANES:] * sin
    ckvb = ckv.astype(BF16)
    q2 = _dot(qn, w.w_uq[...])
    kn = _dot(ckvb, w.w_uk[...])
    vv = _dot(ckvb, w.w_uv[...])
    gq = w.g_qnorm[...] * (QK_HEAD ** -0.5 * LOG2E)
    gk = w.g_knorm[...]
    hw = HEADS * LANES
    q_heads, k_heads = [], []
    for hd in range(HEADS):
        sl = slice(hd * LANES, (hd + 1) * LANES)
        qh = q2[:, sl] * cos + q2[:, hw + hd * LANES:hw + (hd + 1) * LANES] * sin
        q_heads.append(_head_norm(qh, gq).astype(BF16))
        k_heads.append(_head_norm(kn[:, sl] + kr_slot, gk).astype(BF16))
    v_heads = _v_slots(vv)

    u = jax.nn.gelu(_dot(h, w.w_in[:, o_u:o_v]))
    v = _rms(jax.nn.gelu(_dot(h, w.w_in[:, o_v:o_q])), w.g_gmlp_v[...])
    gates = jax.nn.sigmoid(_dot(h, w.w_gate[...]) + w.b_gate[...])
    vb = v.astype(BF16)
    L = chunk_len
    gw = d // GMLP_GROUPS
    row = lax.broadcasted_iota(jnp.int32, (L, L), 0)
    col = lax.broadcasted_iota(jnp.int32, (L, L), 1)
    tril = col <= row
    bias = w.gmlp_bias[...]
    for g in range(GMLP_GROUPS):
        wg = jnp.where(tril, w.gmlp_ws[g, 0:L, 0:L], 0.0).astype(BF16)
        for c in range(tm // L):
            r0 = c * L
            mixed = _dot(wg, vb[r0:r0 + L, g * gw:(g + 1) * gw]) + bias[:, g * gw:(g + 1) * gw]
            oa_scr[r0:r0 + L, g * gw:(g + 1) * gw] = (
                u[r0:r0 + L, g * gw:(g + 1) * gw] * mixed).astype(BF16)
    merged_a = gates[:, :d] * _dot(oa_scr[...], w.w_branch_a[...])
    gate_b = gates[:, d:]
    return merged_a, gate_b, q_heads, k_heads, v_heads, ckv, kr_slot, v


def _v_slots(vv):
    one = (lax.broadcasted_iota(jnp.int32, (1, LANES), 1) == V_HEAD).astype(F32)
    return [(vv[:, hd * LANES:(hd + 1) * LANES] + one).astype(BF16) for hd in range(HEADS)]


def _normalise(acc):
    return acc / acc[:, V_HEAD:V_HEAD + 1]


def _pack_heads(o_heads):
    lane = lax.broadcasted_iota(jnp.int32, o_heads[0].shape, 1)
    parts = [jnp.where(lane < V_HEAD, o_heads[2 * p],
                       pltpu.roll(o_heads[2 * p + 1], V_HEAD, axis=1)).astype(BF16)
             for p in range(HEADS // 2)]
    return jnp.concatenate(parts, axis=1)


def _project(x, gate, merged_a, gate_b, o_b, wb_ref, wo_ref):
    merged = merged_a + gate_b * _dot(o_b, wb_ref[...])
    return x + gate * _dot(merged.astype(BF16), wo_ref[...])


def _finish(x, gate, merged_a, gate_b, o_heads, w):
    return _project(x, gate, merged_a, gate_b, _pack_heads(o_heads), w.w_branch_b, w.w_out)


def _prompt_mixer_kernel(x_ref, m_ref, cos_ref, sin_ref, *rest, offs, n_tiles):
    w = _MixerWeights(*rest[:N_MIXER_WEIGHTS])
    y_ref, ckv_ref, kr_ref, oa_scr, k_scr, v_scr = rest[N_MIXER_WEIGHTS:]
    t = pl.program_id(1)
    x = x_ref[...]
    tm = x.shape[0]
    merged_a, gate_b, q_heads, k_heads, v_heads, ckv, kr_slot, _ = _tokenwise(
        x, m_ref, cos_ref, sin_ref, w, oa_scr, offs=offs, chunk_len=GMLP_CHUNK)
    ckv_ref[...] = ckv
    kr_ref[...] = kr_slot[:, QK_NOPE:QK_HEAD]
    row0 = pl.multiple_of(t * tm, tm)
    for hd in range(HEADS):
        k_scr[hd, pl.ds(row0, tm), :] = k_heads[hd]
        v_scr[hd, pl.ds(row0, tm), :] = v_heads[hd]

    qrow = lax.broadcasted_iota(jnp.int32, (tm, tm), 0) // CHUNK
    kcol = lax.broadcasted_iota(jnp.int32, (tm, tm), 1) // CHUNK
    diag_mask = kcol <= qrow
    ms, accs = [], []
    for hd in range(HEADS):
        s = jnp.where(diag_mask, _dot_nt(q_heads[hd], k_heads[hd]), NEG)
        m0 = jnp.max(s, axis=1, keepdims=True)
        ms.append(m0)
        accs.append(_dot(jnp.exp2(s - m0).astype(BF16), v_heads[hd]))

    for n_prev in range(n_tiles):
        @pl.when(t == n_prev)
        def _(n_prev=n_prev):
            accs_ = list(accs)
            if n_prev:
                prev = slice(0, n_prev * tm)
                for hd in range(HEADS):
                    sp = _dot_nt(q_heads[hd], k_scr[hd, prev, :])
                    m_new = jnp.maximum(ms[hd], jnp.max(sp, axis=1, keepdims=True))
                    pp = jnp.exp2(sp - m_new).astype(BF16)
                    accs_[hd] = jnp.exp2(ms[hd] - m_new) * accs[hd] + _dot(pp, v_scr[hd, prev, :])
            o_heads = [_normalise(a) for a in accs_]
            y_ref[...] = _finish(x, m_ref[2], merged_a, gate_b, o_heads, w)


def _sample_tokens_kernel(x_ref, m_ref, cos_ref, sin_ref, *rest, offs, chunk_len):
    w = _MixerWeights(*rest[:N_MIXER_WEIGHTS])
    q_ref, k_ref, v_ref, ma_ref, gb_ref, ckv_ref, kr_ref, gv_out_ref, oa_scr = rest[N_MIXER_WEIGHTS:]
    merged_a, gate_b, q_heads, k_heads, v_heads, ckv, kr_slot, v_gmlp = _tokenwise(
        x_ref[...], m_ref, cos_ref, sin_ref, w, oa_scr, offs=offs, chunk_len=chunk_len)
    for hd in range(HEADS):
        q_ref[hd] = q_heads[hd]
        k_ref[hd] = k_heads[hd]
        v_ref[hd] = v_heads[hd]
    ma_ref[...] = merged_a
    gb_ref[...] = gate_b
    ckv_ref[...] = ckv
    kr_ref[...] = kr_slot[:, QK_NOPE:QK_HEAD]
    gv_out_ref[...] = v_gmlp


def _sample_project_kernel(x_ref, m_ref, ma_ref, gb_ref, ob_ref, wb_ref, wo_ref, y_ref):
    y_ref[...] = _project(x_ref[...], m_ref[2], ma_ref[...], gb_ref[...], ob_ref[...], wb_ref, wo_ref)


def _sample_attention_kernel(q_ref, k_ref, v_ref, pckv_ref, pkr_ref, wuk_ref, wuv_ref, gk_ref,
                             o_ref, k_scr, v_scr, *, key_chunk):
    past = pckv_ref.shape[0]
    gk = gk_ref[...]

    def build(i, carry):
        r = pl.multiple_of(i * key_chunk, key_chunk)
        cb = pckv_ref[pl.ds(r, key_chunk), :].astype(BF16)
        kn = _dot(cb, wuk_ref[...])
        vs = _v_slots(_dot(cb, wuv_ref[...]))
        kr = pkr_ref[pl.ds(r, key_chunk), :]
        for hd in range(HEADS):
            kh = kn[:, hd * LANES:(hd + 1) * LANES] + kr
            k_scr[hd, pl.ds(r, key_chunk), :] = _head_norm(kh, gk).astype(BF16)
            v_scr[hd, pl.ds(r, key_chunk), :] = vs[hd]
        return carry

    lax.fori_loop(0, past // key_chunk, build, 0)

    o_heads = []
    for hd in range(HEADS):
        q = q_ref[hd]
        sp = _dot_nt(q, k_scr[hd])
        sn = _dot_nt(q, k_ref[hd])
        m = jnp.maximum(jnp.max(sp, axis=1, keepdims=True), jnp.max(sn, axis=1, keepdims=True))
        pp = jnp.exp2(sp - m).astype(BF16)
        pn = jnp.exp2(sn - m).astype(BF16)
        o_heads.append(_normalise(_dot(pp, v_scr[hd]) + _dot(pn, v_ref[hd])))
    o_ref[...] = _pack_heads(o_heads)


def _mixer_weight_specs(wts):
    return [_const_spec(w.shape) for w in wts]


def _prompt_mixer(x, m_arr, cos, sin, wts, offs, batch, seq, tm, kv_rank):
    rows, d = x.shape
    nt = seq // tm
    kern = functools.partial(_prompt_mixer_kernel, offs=offs, n_tiles=nt)
    row_spec = lambda w: pl.BlockSpec((tm, w), lambda b, t: (b * nt + t, 0))
    return pl.pallas_call(
        kern,
        grid=(batch, nt),
        in_specs=[row_spec(d),
                  pl.BlockSpec((None, 3, None, 1, d), lambda b, t: (1, 0, b, 0, 0)),
                  pl.BlockSpec((tm, LANES), lambda b, t: (t, 0)),
                  pl.BlockSpec((tm, LANES), lambda b, t: (t, 0))] + _mixer_weight_specs(wts),
        out_specs=[row_spec(d), row_spec(kv_rank), row_spec(QK_ROPE)],
        out_shape=[jax.ShapeDtypeStruct((rows, d), F32),
                   jax.ShapeDtypeStruct((rows, kv_rank), F32),
                   jax.ShapeDtypeStruct((rows, QK_ROPE), F32)],
        scratch_shapes=[pltpu.VMEM((tm, d), BF16),
                        pltpu.VMEM((HEADS, seq, LANES), BF16),
                        pltpu.VMEM((HEADS, seq, LANES), BF16)],
        compiler_params=pltpu.CompilerParams(dimension_semantics=("arbitrary", "arbitrary"),
                                             vmem_limit_bytes=VMEM_LIMIT),
    )(x, m_arr, cos, sin, *wts)


def _sample_mixer(x, m_rows, cos, sin, past_ckv, past_kr_slot, wts, offs, batch, seq, kv_rank):
    rows, d = x.shape
    past = past_ckv.shape[1]
    key_chunk = 1024 if past % 1024 == 0 else past
    params = pltpu.CompilerParams(dimension_semantics=("arbitrary",), vmem_limit_bytes=VMEM_LIMIT)
    full = lambda *shape: pl.BlockSpec(shape, lambda *_: (0,) * len(shape))
    m_spec = pl.BlockSpec((None, 3, rows, d), lambda *_: (1, 0, 0, 0))
    heads = jax.ShapeDtypeStruct((HEADS, rows, LANES), BF16)

    q, k, v, merged_a, gate_b, ckv, kr, gv = pl.pallas_call(
        functools.partial(_sample_tokens_kernel, offs=offs, chunk_len=min(seq, GMLP_CHUNK)),
        grid=(1,),
        in_specs=[full(rows, d), m_spec, full(rows, LANES), full(rows, LANES)] + _mixer_weight_specs(wts),
        out_specs=[full(HEADS, rows, LANES)] * 3 + [full(rows, d), full(rows, d), full(rows, kv_rank),
                                                    full(rows, QK_ROPE), full(rows, d)],
        out_shape=[heads] * 3 + [jax.ShapeDtypeStruct((rows, d), F32), jax.ShapeDtypeStruct((rows, d), F32),
                                 jax.ShapeDtypeStruct((rows, kv_rank), F32),
                                 jax.ShapeDtypeStruct((rows, QK_ROPE), F32),
                                 jax.ShapeDtypeStruct((rows, d), F32)],
        scratch_shapes=[pltpu.VMEM((rows, d), BF16)],
        compiler_params=params,
    )(x, m_rows, cos, sin, *wts)

    seq_heads = pl.BlockSpec((HEADS, seq, LANES), lambda b: (0, b, 0))
    o_b = pl.pallas_call(
        functools.partial(_sample_attention_kernel, key_chunk=key_chunk),
        grid=(batch,),
        in_specs=[seq_heads, seq_heads, seq_heads,
                  pl.BlockSpec((None, past, kv_rank), lambda b: (b, 0, 0)),
                  pl.BlockSpec((None, past, LANES), lambda b: (b, 0, 0)),
                  _const_spec(wts.w_uk.shape), _const_spec(wts.w_uv.shape), _const_spec(wts.g_knorm.shape)],
        out_specs=pl.BlockSpec((seq, HEADS * V_HEAD), lambda b: (b, 0)),
        out_shape=jax.ShapeDtypeStruct((rows, HEADS * V_HEAD), BF16),
        scratch_shapes=[pltpu.VMEM((HEADS, past, LANES), BF16),
                        pltpu.VMEM((HEADS, past, LANES), BF16)],
        compiler_params=params,
    )(q, k, v, past_ckv, past_kr_slot, wts.w_uk, wts.w_uv, wts.g_knorm)

    y = pl.pallas_call(
        _sample_project_kernel,
        grid=(1,),
        in_specs=[full(rows, d), m_spec, full(rows, d), full(rows, d), full(rows, HEADS * V_HEAD),
                  full(*wts.w_branch_b.shape), full(*wts.w_out.shape)],
        out_specs=full(rows, d),
        out_shape=jax.ShapeDtypeStruct((rows, d), F32),
        compiler_params=params,
    )(x, m_rows, merged_a, gate_b, o_b, wts.w_branch_b, wts.w_out)
    return y, ckv, kr, gv


def _rope_tables(pos):
    half = QK_ROPE // 2
    freqs = ROPE_THETA ** (-jnp.arange(half, dtype=F32) / half)
    ang = pos[:, None] * freqs[None, :]
    cos, sin = jnp.cos(ang), jnp.sin(ang)
    n = pos.shape[0]
    pad = LANES - QK_HEAD
    c = jnp.concatenate([jnp.ones((n, QK_NOPE), F32), cos, cos, jnp.zeros((n, pad), F32)], axis=1)
    s = jnp.concatenate([jnp.zeros((n, QK_NOPE), F32), -sin, sin, jnp.zeros((n, pad), F32)], axis=1)
    return c, s


def _rope_slot(w):
    half = QK_ROPE // 2
    lead = w.shape[:-1]
    z0 = jnp.zeros(lead + (QK_NOPE,), w.dtype)
    z1 = jnp.zeros(lead + (LANES - QK_HEAD,), w.dtype)
    slot = jnp.concatenate([z0, w, z1], axis=-1)
    swapped = jnp.concatenate([z0, w[..., half:], w[..., :half], z1], axis=-1)
    return slot, swapped


def _layer_weights(g_mix, w_in, g_gmlp_v, gmlp_ws, gmlp_b, g_q_lat, w_uq, g_kv_lat, w_uk, w_uv,
                   g_qnorm, g_knorm, b_gate, w_branch_a, w_branch_b, w_out, chunk_len):
    d = w_in.shape[0]
    d_a = g_gmlp_v.shape[0]
    q_rank = g_q_lat.shape[0]
    kv_rank = g_kv_lat.shape[0]
    offs = _in_offsets(d_a, q_rank, kv_rank)
    o_kr = offs[-1]
    win_lead = w_in[:, :o_kr].astype(BF16)
    kr_slot, kr_swapped = _rope_slot(w_in[:, o_kr:o_kr + QK_ROPE])
    w_kr = jnp.concatenate([kr_slot, kr_swapped], axis=1).astype(BF16)
    w_gate = w_in[:, o_kr + QK_ROPE:].astype(BF16)

    uq3 = w_uq.reshape(q_rank, HEADS, QK_HEAD)
    rope_slot, rope_swapped = _rope_slot(uq3[..., QK_NOPE:])
    nope = jnp.concatenate([uq3[..., :QK_NOPE], jnp.zeros((q_rank, HEADS, LANES - QK_NOPE), F32)], -1)
    wuq_p = jnp.concatenate([(nope + rope_slot).reshape(q_rank, HEADS * LANES),
                             rope_swapped.reshape(q_rank, HEADS * LANES)], axis=1).astype(BF16)
    uk3 = w_uk.reshape(kv_rank, HEADS, QK_NOPE)
    wuk_p = jnp.concatenate([uk3, jnp.zeros((kv_rank, HEADS, LANES - QK_NOPE), F32)], -1)
    wuk_p = wuk_p.reshape(kv_rank, HEADS * LANES).astype(BF16)
    uv3 = w_uv.reshape(kv_rank, HEADS, V_HEAD)
    wuv_p = jnp.concatenate([uv3, jnp.zeros((kv_rank, HEADS, LANES - V_HEAD), F32)], -1)
    wuv_p = wuv_p.reshape(kv_rank, HEADS * LANES).astype(BF16)
    pad = jnp.zeros((LANES - QK_HEAD,), F32)
    gq = jnp.concatenate([g_qnorm, pad]).reshape(1, LANES)
    gk = jnp.concatenate([g_knorm, pad]).reshape(1, LANES)
    gw = d_a // GMLP_GROUPS
    bias = jnp.repeat(gmlp_b[:, :chunk_len].T, gw, axis=1)
    wts = _MixerWeights(
        g_mix=g_mix.reshape(1, d), w_in=win_lead, w_kr=w_kr, w_gate=w_gate,
        g_gmlp_v=g_gmlp_v.reshape(1, d_a), gmlp_ws=gmlp_ws, gmlp_bias=bias,
        g_q_lat=g_q_lat.reshape(1, q_rank), w_uq=wuq_p, g_kv_lat=g_kv_lat.reshape(1, kv_rank),
        w_uk=wuk_p, w_uv=wuv_p, g_qnorm=gq, g_knorm=gk, b_gate=b_gate.reshape(1, -1),
        w_branch_a=w_branch_a.astype(BF16), w_branch_b=w_branch_b.astype(BF16), w_out=w_out.astype(BF16))
    return wts, offs


def _prompt_tile(seq):
    for tm in (512, 256, 128):
        if seq % tm == 0:
            return tm
    raise ValueError("prompt length must be a multiple of 128")


def _ffn_tile(seq):
    for tm in (1024, 512, 256, 128):
        if seq % tm == 0:
            return tm
    raise ValueError("prompt length must be a multiple of 128")


def kernel(x_prompt, x_sample, c_prompt, c_sample, cache_ckv, cache_krope, w_mod, b_mod, g_ffn1, w_ffn1_up, w_ffn1_down, g_mix, w_in, g_gmlp_v, gmlp_ws, gmlp_b, g_q_lat, w_uq, g_kv_lat, w_uk, w_uv, g_qnorm, g_knorm, b_gate, w_branch_a, w_branch_b, w_out, g_ffn2, w_ffn2_up, w_ffn2_down):
    bp, tp, d = x_prompt.shape
    bs, ts, _ = x_sample.shape
    depth = w_mod.shape[0]
    past = cache_ckv.shape[2]
    kv_rank = g_kv_lat.shape[1]
    tm = _prompt_tile(tp)
    assert ts % 8 == 0 and ts <= GMLP_CHUNK and (bs * ts) % 8 == 0

    cos_p, sin_p = _rope_tables(jnp.arange(tp, dtype=F32))
    cos_s, sin_s = _rope_tables(jnp.arange(ts, dtype=F32) + jnp.float32(past))
    c_all = jnp.concatenate([c_prompt, c_sample], axis=0)

    xp = x_prompt.reshape(bp * tp, d)
    xs = x_sample.reshape(bs * ts, d)
    outs = {k: [] for k in ("ckv_p", "kr_p", "ckv_s", "kr_s", "vg_s")}
    for l in range(depth):
        m = _modulation(c_all, w_mod[l], b_mod[l])
        m5 = m.reshape(bp + bs, 3, 3, d).transpose(1, 2, 0, 3)
        m_p = m5[:, :, :bp].reshape(3, 3, bp, 1, d)
        m_s_rows = jnp.repeat(m5[:, :, bp:], ts, axis=2)
        nt = tp // tm
        rows_s = bs * ts

        tm_ffn = _ffn_tile(tp)
        nt_ffn = tp // tm_ffn

        def ffn_pair(sub, g, w_up, w_down, xp, xs):
            w_up_b, w_down_b = w_up.astype(BF16), w_down.astype(BF16)
            p_spec = pl.BlockSpec((None, 3, None, 1, d), lambda i: (sub, 0, i // nt_ffn, 0, 0))
            s_spec = pl.BlockSpec((None, 3, rows_s, d), lambda i: (sub, 0, 0, 0))
            return (_ffn(xp, p_spec, m_p, g, w_up_b, w_down_b, tm_ffn),
                    _ffn(xs, s_spec, m_s_rows, g, w_up_b, w_down_b, rows_s))

        xp, xs = ffn_pair(0, g_ffn1[l], w_ffn1_up[l], w_ffn1_down[l], xp, xs)

        layer_args = (g_mix[l], w_in[l], g_gmlp_v[l], gmlp_ws[l], gmlp_b[l], g_q_lat[l], w_uq[l],
                      g_kv_lat[l], w_uk[l], w_uv[l], g_qnorm[l], g_knorm[l], b_gate[l],
                      w_branch_a[l], w_branch_b[l], w_out[l])
        wts_p, offs = _layer_weights(*layer_args, chunk_len=GMLP_CHUNK)
        xp, ckv_p, kr_p = _prompt_mixer(xp, m_p, cos_p, sin_p, wts_p, offs, bp, tp, tm, kv_rank)
        wts_s, _ = _layer_weights(*layer_args, chunk_len=min(ts, GMLP_CHUNK))
        past_kr_slot, _ = _rope_slot(cache_krope[l])
        xs, ckv_s, kr_s, vg_s = _sample_mixer(xs, m_s_rows, jnp.tile(cos_s, (bs, 1)), jnp.tile(sin_s, (bs, 1)),
                                              cache_ckv[l], past_kr_slot, wts_s, offs, bs, ts, kv_rank)

        xp, xs = ffn_pair(2, g_ffn2[l], w_ffn2_up[l], w_ffn2_down[l], xp, xs)

        outs["ckv_p"].append(ckv_p.reshape(bp, tp, kv_rank))
        outs["kr_p"].append(kr_p.reshape(bp, tp, QK_ROPE))
        outs["ckv_s"].append(ckv_s.reshape(bs, ts, kv_rank))
        outs["kr_s"].append(kr_s.reshape(bs, ts, QK_ROPE))
        outs["vg_s"].append(vg_s.reshape(bs, ts, d))
    return (xp.reshape(bp, tp, d), xs.reshape(bs, ts, d),
            jnp.stack(outs["ckv_p"], 0), jnp.stack(outs["kr_p"], 0),
            jnp.stack(outs["ckv_s"], 0), jnp.stack(outs["kr_s"], 0), jnp.stack(outs["vg_s"], 0))
```
